```python
import math
import jax, jax.numpy as jnp
from jax import lax
import numpy as np

D_MODEL = 2048
BATCH = 2
SEQ = 4096
DEPTH = 2

D_MIX = D_MODEL
RW_HEADS = 12
RW_HD = 64
RW_W = RW_HEADS * RW_HD
RW_DECAY_LORA = 64
RW_AAA_LORA = 64
RW_GATE_LORA = 128
RW_GN_EPS = 64e-5
DA_HEADS = 6
DA_HD = 64
DA_W = DA_HEADS * 2 * DA_HD
DA_SUBLN_EPS = 1e-5
Q_BLOCK = 128
SSM_W = D_MIX - RW_W - DA_W
SSM_GROUP = 16
SSM_GROUPS = SSM_W // SSM_GROUP
SSM_STATE = 64
RW_COLS = 3 * RW_W + RW_DECAY_LORA + RW_AAA_LORA + RW_GATE_LORA
DA_COLS = 3 * DA_W
IN_COLS = RW_COLS + DA_COLS + SSM_W
D_FF = 5504
NORM_EPS = 1e-6

kernel_name = "hybrid_rwkv7_diffattn_s5_macaron"


def rms_norm(x, g):
    xf = x.astype(jnp.float32)
    y = xf * lax.rsqrt(jnp.mean(xf * xf, axis=-1, keepdims=True) + NORM_EPS)
    return y.astype(x.dtype) * g


def swiglu(h, w_gu, w_down):
    gate, up = jnp.split(h @ w_gu, 2, axis=-1)
    return (jax.nn.silu(gate) * up) @ w_down


def rwkv7_mix(z, mu, w0, w2, a0, a2, g2, k_k, k_a, r_k, gn_w, gn_b):
    f32 = jnp.float32
    b_, t_, _ = z.shape
    z_prev = jnp.pad(z, ((0, 0), (1, 0), (0, 0)))[:, :-1]
    z = z + (z_prev - z) * mu
    o1, o2, o3 = RW_W, 2 * RW_W, 3 * RW_W
    o4, o5 = o3 + RW_DECAY_LORA, o3 + RW_DECAY_LORA + RW_AAA_LORA
    r, k, v = z[..., :o1], z[..., o1:o2], z[..., o2:o3]
    zw, za, zg = z[..., o3:o4], z[..., o4:o5], z[..., o5:]
    w = -jax.nn.softplus(-(w0 + jnp.tanh(zw) @ w2)) - 0.5
    decay = jnp.exp(-jnp.exp(w.astype(f32)))
    a = jax.nn.sigmoid(a0 + za @ a2)
    g = jax.nn.sigmoid(zg) @ g2
    hs = lambda t: t.reshape(b_, t_, RW_HEADS, RW_HD).astype(f32)
    r, k, v, decay, a = hs(r), hs(k), hs(v), hs(decay), hs(a)
    kk = k * k_k.reshape(RW_HEADS, RW_HD).astype(f32)
    kk = kk / jnp.maximum(jnp.sqrt(jnp.sum(kk * kk, axis=-1, keepdims=True)), 1e-12)
    k = k * (1.0 + (a - 1.0) * k_a.reshape(RW_HEADS, RW_HD).astype(f32))
    rem_a = -kk
    rem_b = kk * a

    def step(S, inp):
        r_t, w_t, k_t, v_t, a_t, b_t = inp
        sa = jnp.einsum('bhvk,bhk->bhv', S, a_t)
        S = S * w_t[:, :, None, :] + sa[..., None] * b_t[:, :, None, :] + v_t[..., None] * k_t[:, :, None, :]
        return S, jnp.einsum('bhvk,bhk->bhv', S, r_t)

    S0 = jnp.zeros((b_, RW_HEADS, RW_HD, RW_HD), f32)
    seq_first = tuple(t.transpose(1, 0, 2, 3) for t in (r, decay, k, v, rem_a, rem_b))
    _, y = lax.scan(step, S0, seq_first)
    y = y.transpose(1, 0, 2, 3)
    mean = jnp.mean(y, axis=-1, keepdims=True)
    var = jnp.mean(jnp.square(y - mean), axis=-1, keepdims=True)
    y = ((y - mean) * lax.rsqrt(var + RW_GN_EPS)).reshape(b_, t_, RW_W) * gn_w + gn_b
    bonus = jnp.sum(r * k * r_k.astype(f32), axis=-1, keepdims=True) * v
    y = (y + bonus.reshape(b_, t_, RW_W)) * g
    return y.astype(z.dtype)


def diff_attn_mix(z, lq1, lk1, lq2, lk2, subln_w, lam_init):
    f32 = jnp.float32
    b_, t_, _ = z.shape
    q = z[..., :DA_W].reshape(b_, t_, DA_HEADS, 2, DA_HD).astype(f32)
    k = z[..., DA_W:2 * DA_W].reshape(b_, t_, DA_HEADS, 2, DA_HD).astype(f32)
    v = z[..., 2 * DA_W:].reshape(b_, t_, DA_HEADS, 2 * DA_HD).astype(f32)
    lam = (jnp.exp(jnp.sum(lq1.astype(f32) * lk1.astype(f32)))
           - jnp.exp(jnp.sum(lq2.astype(f32) * lk2.astype(f32))) + lam_init)
    nb = t_ // Q_BLOCK
    qb = q.reshape(b_, nb, Q_BLOCK, DA_HEADS, 2, DA_HD).transpose(1, 0, 2, 3, 4, 5) * (DA_HD ** -0.5)
    kpos = jnp.arange(t_)

    def block(args):
        q_blk, blk = args
        s = jnp.einsum('bqhcd,bkhcd->bhcqk', q_blk, k)
        qpos = blk * Q_BLOCK + jnp.arange(Q_BLOCK)
        s = jnp.where(kpos[None, :] <= qpos[:, None], s, -jnp.inf)
        p = jax.nn.softmax(s, axis=-1)
        attn = p[:, :, 0] - lam * p[:, :, 1]
        return jnp.einsum('bhqk,bkhe->bqhe', attn, v)

    o = lax.map(block, (qb, jnp.arange(nb)))
    o = o.transpose(1, 0, 2, 3, 4).reshape(b_, t_, DA_HEADS, 2 * DA_HD)
    o = o * lax.rsqrt(jnp.mean(o * o, axis=-1, keepdims=True) + DA_SUBLN_EPS) * subln_w.astype(f32)
    o = o * (1.0 - lam_init)
    return o.reshape(b_, t_, DA_W).astype(z.dtype)


def _complex_linear_combine(e1, e2):
    a1r, a1i, b1r, b1i = e1
    a2r, a2i, b2r, b2i = e2
    return (a2r * a1r - a2i * a1i,
            a2r * a1i + a2i * a1r,
            a2r * b1r - a2i * b1i + b2r,
            a2r * b1i + a2i * b1r + b2i)


def s5_mix(u, a_re, a_im, log_dt, b_re, b_im, c_re, c_im, d_skip, w_glu, b_glu):
    f32 = jnp.float32
    b_, t_, _ = u.shape
    uf = u.astype(f32).reshape(b_, t_, SSM_GROUPS, SSM_GROUP)
    dt = jnp.exp(log_dt.astype(f32))[:, None]
    ar, ai = a_re.astype(f32), a_im.astype(f32)
    mag = jnp.exp(dt * ar)
    abar_r, abar_i = mag * jnp.cos(dt * ai), mag * jnp.sin(dt * ai)
    den = ar * ar + ai * ai
    nr, ni = abar_r - 1.0, abar_i
    coef_r, coef_i = (nr * ar + ni * ai) / den, (ni * ar - nr * ai) / den
    br, bi = b_re.astype(f32), b_im.astype(f32)
    bbar_r = coef_r[..., None] * br - coef_i[..., None] * bi
    bbar_i = coef_r[..., None] * bi + coef_i[..., None] * br
    bu_r = jnp.einsum('btgc,gnc->btgn', uf, bbar_r)
    bu_i = jnp.einsum('btgc,gnc->btgn', uf, bbar_i)
    shape = bu_r.shape
    elems = (jnp.broadcast_to(abar_r, shape), jnp.broadcast_to(abar_i, shape), bu_r, bu_i)
    _, _, xr, xi = lax.associative_scan(_complex_linear_combine, elems, axis=1)
    y = (jnp.einsum('btgn,gcn->btgc', xr, c_re.astype(f32))
         - jnp.einsum('btgn,gcn->btgc', xi, c_im.astype(f32)))
    y = y.reshape(b_, t_, SSM_W) + d_skip.astype(f32) * uf.reshape(b_, t_, SSM_W)
    y = jax.nn.gelu(y)
    y = y * jax.nn.sigmoid(y @ w_glu.astype(f32) + b_glu.astype(f32))
    return y.astype(u.dtype)


def setup_inputs(seed: int = 0) -> dict:
    key = jax.random.key(seed)
    keys = iter(jax.random.split(key, 64))
    f32 = jnp.float32
    L = DEPTH
    nrm = lambda shape, s: s * jax.random.normal(next(keys), shape, f32)
    uni = lambda shape, lo, hi: jax.random.uniform(next(keys), shape, f32, lo, hi)
    gain = lambda shape: 1.0 + 0.05 * jax.random.normal(next(keys), shape, f32)
    a_im0 = jnp.broadcast_to(jnp.pi * jnp.arange(SSM_STATE, dtype=f32), (L, SSM_GROUPS, SSM_STATE))
    return {
        "x": nrm((BATCH, SEQ, D_MODEL), 1.0),
        "ffn1_pre_g": gain((L, D_MODEL)),
        "ffn1_w_gu": nrm((L, D_MODEL, 2 * D_FF), D_MODEL ** -0.5),
        "ffn1_w_down": nrm((L, D_FF, D_MODEL), D_FF ** -0.5),
        "ffn1_post_g": gain((L, D_MODEL)),
        "mix_pre_g": gain((L, D_MODEL)),
        "w_in": nrm((L, D_MODEL, IN_COLS), D_MODEL ** -0.5),
        "rw_mu": uni((L, RW_COLS), 0.0, 1.0),
        "rw_w0": uni((L, RW_W), -5.0, -0.5),
        "rw_w2": nrm((L, RW_DECAY_LORA, RW_W), 0.1),
        "rw_a0": nrm((L, RW_W), 0.1),
        "rw_a2": nrm((L, RW_AAA_LORA, RW_W), 0.5 * RW_AAA_LORA ** -0.5),
        "rw_g2": nrm((L, RW_GATE_LORA, RW_W), RW_GATE_LORA ** -0.5),
        "rw_k_k": 0.85 + nrm((L, RW_W), 0.05),
        "rw_k_a": 1.0 + nrm((L, RW_W), 0.05),
        "rw_r_k": nrm((L, RW_HEADS, RW_HD), 0.1),
        "rw_gn_w": gain((L, RW_W)),
        "rw_gn_b": nrm((L, RW_W), 0.02),
        "da_lq1": nrm((L, DA_HD), 0.1),
        "da_lk1": nrm((L, DA_HD), 0.1),
        "da_lq2": nrm((L, DA_HD), 0.1),
        "da_lk2": nrm((L, DA_HD), 0.1),
        "da_subln_w": gain((L, 2 * DA_HD)),
        "ssm_a_re": -0.5 + nrm((L, SSM_GROUPS, SSM_STATE), 0.01),
        "ssm_a_im": a_im0 + nrm((L, SSM_GROUPS, SSM_STATE), 0.01),
        "ssm_log_dt": uni((L, SSM_GROUPS), math.log(0.001), math.log(0.1)),
        "ssm_b_re": nrm((L, SSM_GROUPS, SSM_STATE, SSM_GROUP), (0.5 / SSM_GROUP) ** 0.5),
        "ssm_b_im": nrm((L, SSM_GROUPS, SSM_STATE, SSM_GROUP), (0.5 / SSM_GROUP) ** 0.5),
        "ssm_c_re": nrm((L, SSM_GROUPS, SSM_GROUP, SSM_STATE), (1.0 / SSM_STATE) ** 0.5),
        "ssm_c_im": nrm((L, SSM_GROUPS, SSM_GROUP, SSM_STATE), (1.0 / SSM_STATE) ** 0.5),
        "ssm_d": nrm((L, SSM_W), 1.0),
        "ssm_w_glu": nrm((L, SSM_W, SSM_W), SSM_W ** -0.5),
        "ssm_b_glu": nrm((L, SSM_W), 0.02),
        "w_out": nrm((L, D_MIX, D_MODEL), D_MIX ** -0.5),
        "mix_post_g": gain((L, D_MODEL)),
        "ffn2_pre_g": gain((L, D_MODEL)),
        "ffn2_w_gu": nrm((L, D_MODEL, 2 * D_FF), D_MODEL ** -0.5),
        "ffn2_w_down": nrm((L, D_FF, D_MODEL), D_FF ** -0.5),
        "ffn2_post_g": gain((L, D_MODEL)),
    }


def reference(x, ffn1_pre_g, ffn1_w_gu, ffn1_w_down, ffn1_post_g, mix_pre_g, w_in,
              rw_mu, rw_w0, rw_w2, rw_a0, rw_a2, rw_g2, rw_k_k, rw_k_a, rw_r_k, rw_gn_w, rw_gn_b,
              da_lq1, da_lk1, da_lq2, da_lk2, da_subln_w,
              ssm_a_re, ssm_a_im, ssm_log_dt, ssm_b_re, ssm_b_im, ssm_c_re, ssm_c_im, ssm_d,
              ssm_w_glu, ssm_b_glu, w_out, mix_post_g,
              ffn2_pre_g, ffn2_w_gu, ffn2_w_down, ffn2_post_g):
    for l in range(DEPTH):
        h = swiglu(rms_norm(x, ffn1_pre_g[l]), ffn1_w_gu[l], ffn1_w_down[l])
        x = x + 0.5 * rms_norm(h, ffn1_post_g[l])
        z = rms_norm(x, mix_pre_g[l]) @ w_in[l]
        y_rw = rwkv7_mix(z[..., :RW_COLS], rw_mu[l], rw_w0[l], rw_w2[l], rw_a0[l], rw_a2[l],
                         rw_g2[l], rw_k_k[l], rw_k_a[l], rw_r_k[l], rw_gn_w[l], rw_gn_b[l])
        lam_init = 0.8 - 0.6 * math.exp(-0.3 * l)
        y_da = diff_attn_mix(z[..., RW_COLS:RW_COLS + DA_COLS], da_lq1[l], da_lk1[l],
                             da_lq2[l], da_lk2[l], da_subln_w[l], lam_init)
        y_ss = s5_mix(z[..., RW_COLS + DA_COLS:], ssm_a_re[l], ssm_a_im[l], ssm_log_dt[l],
                      ssm_b_re[l], ssm_b_im[l], ssm_c_re[l], ssm_c_im[l], ssm_d[l],
                      ssm_w_glu[l], ssm_b_glu[l])
        y = jnp.concatenate([y_rw, y_da, y_ss], axis=-1) @ w_out[l]
        x = x + rms_norm(y, mix_post_g[l])
        h = swiglu(rms_norm(x, ffn2_pre_g[l]), ffn2_w_gu[l], ffn2_w_down[l])
        x = x + 0.5 * rms_norm(h, ffn2_post_g[l])
    return x
```

```python
import functools
import math

import jax
import jax.numpy as jnp
from jax import lax
from jax.experimental import pallas as pl
from jax.experimental.pallas import tpu as pltpu

F32 = jnp.float32
BF16 = jnp.bfloat16
HIGHEST = lax.Precision.HIGHEST

D_MODEL = 2048
DEPTH = 2
RW_HEADS = 12
RW_HD = 64
RW_W = RW_HEADS * RW_HD
RW_DECAY_LORA = 64
RW_AAA_LORA = 64
RW_GATE_LORA = 128
RW_GN_EPS = 64e-5
DA_HEADS = 6
DA_HD = 64
DA_W = DA_HEADS * 2 * DA_HD
DA_SUBLN_EPS = 1e-5
SSM_W = D_MODEL - RW_W - DA_W
SSM_GROUP = 16
SSM_GROUPS = SSM_W // SSM_GROUP
SSM_STATE = 64
SSM_N = SSM_GROUPS * SSM_STATE
RW_COLS = 3 * RW_W + RW_DECAY_LORA + RW_AAA_LORA + RW_GATE_LORA
DA_COLS = 3 * DA_W
D_FF = 5504
NORM_EPS = 1e-6

LANES = 128
SUBLANES = 8
VMEM_LIMIT = 56 * 1024 * 1024
FF_TILE = 512
D_FF_PAD = -(-D_FF // FF_TILE) * FF_TILE
IN_TILE = 512
SSM_COL0 = -(-(RW_COLS + DA_COLS) // IN_TILE) * IN_TILE
IN_COLS_PAD = SSM_COL0 + SSM_W
ROW_TILE = 512
RW_CHUNK = 64
RW_TSTEP = 256
ATT_TQ = 256
ATT_TK = 256
SSM_TSTEP = 256
PREP_TSTEP = 256
NEG_BIG = -1e30


def _dot(a, b, precision=None):
    return jnp.dot(a, b, preferred_element_type=F32, precision=precision)


def _dot_nt(a, b, precision=None):
    return lax.dot_general(a, b, (((1,), (1,)), ((), ())),
                           preferred_element_type=F32, precision=precision)


def _rms(x, g, eps):
    return x * lax.rsqrt(jnp.mean(x * x, axis=-1, keepdims=True) + eps) * g


def _params(sem):
    return pltpu.CompilerParams(dimension_semantics=sem, vmem_limit_bytes=VMEM_LIMIT)


def _ffn_kernel(x_ref, pre_ref, wg_ref, wu_ref, wd_ref, post_ref, o_ref, xn_ref, acc_ref):
    j = pl.program_id(1)

    @pl.when(j == 0)
    def _():
        xn_ref[...] = _rms(x_ref[...], pre_ref[...], NORM_EPS).astype(BF16)
        acc_ref[...] = jnp.zeros_like(acc_ref)

    xn = xn_ref[...]
    gate = _dot(xn, wg_ref[...])
    up = _dot(xn, wu_ref[...])
    h = (gate * jax.nn.sigmoid(gate) * up).astype(BF16)
    acc_ref[...] += _dot(h, wd_ref[...])

    @pl.when(j == pl.num_programs(1) - 1)
    def _():
        o_ref[...] = x_ref[...] + 0.5 * _rms(acc_ref[...], post_ref[...], NORM_EPS)


def _ffn(x, pre_g, wg, wu, wd, post_g):
    m, d = x.shape
    fp = wg.shape[1]
    tm, tf = min(ROW_TILE, m), FF_TILE
    return pl.pallas_call(
        _ffn_kernel,
        out_shape=jax.ShapeDtypeStruct((m, d), F32),
        grid=(m // tm, fp // tf),
        in_specs=[
            pl.BlockSpec((tm, d), lambda i, j: (i, 0)),
            pl.BlockSpec((1, d), lambda i, j: (0, 0)),
            pl.BlockSpec((d, tf), lambda i, j: (0, j)),
            pl.BlockSpec((d, tf), lambda i, j: (0, j)),
            pl.BlockSpec((tf, d), lambda i, j: (j, 0)),
            pl.BlockSpec((1, d), lambda i, j: (0, 0)),
        ],
        out_specs=pl.BlockSpec((tm, d), lambda i, j: (i, 0)),
        scratch_shapes=[pltpu.VMEM((tm, d), BF16), pltpu.VMEM((tm, d), F32)],
        compiler_params=_params(("parallel", "arbitrary")),
        name="ffn",
    )(x, pre_g, wg, wu, wd, post_g)


def _inproj_kernel(x_ref, g_ref, w_ref, o_ref, xn_ref):
    @pl.when(pl.program_id(1) == 0)
    def _():
        xn_ref[...] = _rms(x_ref[...], g_ref[...], NORM_EPS).astype(BF16)

    o_ref[...] = _dot(xn_ref[...], w_ref[...])


def _inproj(x, g, w):
    m, d = x.shape
    n = w.shape[1]
    tm, tn = min(ROW_TILE, m), IN_TILE
    return pl.pallas_call(
        _inproj_kernel,
        out_shape=jax.ShapeDtypeStruct((m, n), F32),
        grid=(m // tm, n // tn),
        in_specs=[
            pl.BlockSpec((tm, d), lambda i, j: (i, 0)),
            pl.BlockSpec((1, d), lambda i, j: (0, 0)),
            pl.BlockSpec((d, tn), lambda i, j: (0, j)),
        ],
        out_specs=pl.BlockSpec((tm, tn), lambda i, j: (i, j)),
        scratch_shapes=[pltpu.VMEM((tm, d), BF16)],
        compiler_params=_params(("parallel", "arbitrary")),
        name="inproj",
    )(x, g, w)


def _softplus(x):
    return jnp.maximum(x, 0.0) + jnp.log(1.0 + jnp.exp(-jnp.abs(x)))


def _rwprep_kernel(z_ref, zp_ref, mu_ref, w0_ref, w2_ref, a0_ref, a2_ref, g2_ref, kk_ref, ka_ref,
                   r_o, lw_o, k_o, v_o, kk_o, b_o, g_o):
    i = pl.program_id(1)
    z = z_ref[0]
    tt = z.shape[0]
    last_prev = jnp.where(i == 0, 0.0, zp_ref[0, SUBLANES - 1:SUBLANES, :])
    row = lax.broadcasted_iota(jnp.int32, z.shape, 0)
    z_prev = jnp.where(row == 0, last_prev, pltpu.roll(z, 1, 0))
    z = z + (z_prev - z) * mu_ref[...]
    o1, o2, o3 = RW_W, 2 * RW_W, 3 * RW_W
    o4, o5 = o3 + RW_DECAY_LORA, o3 + RW_DECAY_LORA + RW_AAA_LORA
    r, k, v = z[:, :o1], z[:, o1:o2], z[:, o2:o3]
    zw, za, zg = z[:, o3:o4], z[:, o4:o5], z[:, o5:]
    w = -_softplus(-(w0_ref[...] + _dot(jnp.tanh(zw), w2_ref[...], HIGHEST))) - 0.5
    lw = -jnp.exp(w)
    a = jax.nn.sigmoid(a0_ref[...] + _dot(za, a2_ref[...], HIGHEST))
    g = _dot(jax.nn.sigmoid(zg), g2_ref[...], HIGHEST)
    kk = k * kk_ref[...]
    k = k * (1.0 + (a - 1.0) * ka_ref[...])
    for h in range(RW_HEADS):
        sl = slice(h * RW_HD, (h + 1) * RW_HD)
        kk_h = kk[:, sl]
        nrm = jnp.sqrt(jnp.sum(kk_h * kk_h, axis=-1, keepdims=True))
        kk_h = kk_h / jnp.maximum(nrm, 1e-12)
        r_o[0, h] = r[:, sl]
        lw_o[0, h] = lw[:, sl]
        k_o[0, h] = k[:, sl]
        v_o[0, h] = v[:, sl]
        kk_o[0, h] = kk_h
        b_o[0, h] = kk_h * a[:, sl]
        g_o[0, h] = g[:, sl]


def _rwprep(z, mu, w0, w2, a0, a2, g2, k_k, k_a):
    b_, t_, _ = z.shape
    tt = min(PREP_TSTEP, t_)
    nprev = tt // SUBLANES
    full = lambda shape: pl.BlockSpec(shape, lambda b, i: (0,) * len(shape))
    head_out = jax.ShapeDtypeStruct((b_, RW_HEADS, t_, RW_HD), F32)
    head_spec = pl.BlockSpec((1, RW_HEADS, tt, RW_HD), lambda b, i: (b, 0, i, 0))
    return pl.pallas_call(
        _rwprep_kernel,
        out_shape=[head_out] * 7,
        grid=(b_, t_ // tt),
        in_specs=[
            pl.BlockSpec((1, tt, RW_COLS), lambda b, i: (b, i, 0)),
            pl.BlockSpec((1, SUBLANES, RW_COLS), lambda b, i: (b, jnp.maximum(i * nprev - 1, 0), 0)),
            full((1, RW_COLS)), full((1, RW_W)), full((RW_DECAY_LORA, RW_W)), full((1, RW_W)),
            full((RW_AAA_LORA, RW_W)), full((RW_GATE_LORA, RW_W)), full((1, RW_W)), full((1, RW_W)),
        ],
        out_specs=[head_spec] * 7,
        compiler_params=_params(("parallel", "arbitrary")),
        name="rwkv_features",
    )(z, z, mu, w0, w2, a0, a2, g2, k_k, k_a)


def _unit_lower_inverse(n_strict, row, col):
    c = n_strict.shape[0]
    eye = (row == col).astype(F32)
    base = 8
    a0 = jnp.where((row // base) == (col // base), n_strict, 0.0)
    a2 = _dot(a0, a0, HIGHEST)
    a4 = _dot(a2, a2, HIGHEST)
    x = _dot(_dot(eye + a0, eye + a2, HIGHEST), eye + a4, HIGHEST)
    m = base
    while m < c:
        e = jnp.where(((row // (2 * m)) == (col // (2 * m))) & ((row // m) != (col // m)), n_strict, 0.0)
        x = x + _dot(_dot(x, e, HIGHEST), x, HIGHEST)
        m *= 2
    return x


def _rwscan_chunk(r, lw, k, v, kk, b, s, tri_incl, strict, incl, row, col):
    c = r.shape[0]
    lc = _dot(tri_incl, lw, HIGHEST)
    e_inc = jnp.exp(lc)
    e_exc = jnp.exp(lc - lw)
    e_inv = jnp.exp(-lc)
    at = -kk * e_exc
    rt = r * e_inc
    bt = b * e_inv
    kt = k * e_inv
    a_ab = jnp.where(strict, _dot_nt(at, bt, HIGHEST), 0.0)
    a_ak = jnp.where(strict, _dot_nt(at, kt, HIGHEST), 0.0)
    r_b = jnp.where(incl, _dot_nt(rt, bt, HIGHEST), 0.0)
    r_k = jnp.where(incl, _dot_nt(rt, kt, HIGHEST), 0.0)
    x = _unit_lower_inverse(a_ab, row, col)
    u = _dot(x, _dot_nt(at, s, HIGHEST) + _dot(a_ak, v, HIGHEST), HIGHEST)
    y = _dot_nt(rt, s, HIGHEST) + _dot(r_b, u, HIGHEST) + _dot(r_k, v, HIGHEST)
    e_end = jnp.exp(lc[c - 1:c, :] - lc)
    s_new = (s * e_inc[c - 1:c, :] + _dot(u.T, b * e_end, HIGHEST) + _dot(v.T, k * e_end, HIGHEST))
    return y, s_new


def _rwscan_kernel(r_ref, lw_ref, k_ref, v_ref, kk_ref, b_ref, g_ref, rk_ref, gnw_ref, gnb_ref,
                   o_ref, s_ref):
    @pl.when(pl.program_id(2) == 0)
    def _():
        s_ref[...] = jnp.zeros_like(s_ref)

    c = RW_CHUNK
    nheads = r_ref.shape[1]
    tstep = r_ref.shape[2]
    row = lax.broadcasted_iota(jnp.int32, (c, c), 0)
    col = lax.broadcasted_iota(jnp.int32, (c, c), 1)
    strict = row > col
    incl = row >= col
    tri_incl = incl.astype(F32)
    for ci in range(tstep // c):
        rows = pl.ds(ci * c, c)
        outs = []
        for hh in range(nheads):
            r = r_ref[0, hh, rows, :]
            k = k_ref[0, hh, rows, :]
            v = v_ref[0, hh, rows, :]
            y, s_new = _rwscan_chunk(r, lw_ref[0, hh, rows, :], k, v, kk_ref[0, hh, rows, :],
                                     b_ref[0, hh, rows, :], s_ref[hh], tri_incl, strict, incl, row, col)
            s_ref[hh] = s_new
            mean = jnp.mean(y, axis=-1, keepdims=True)
            var = jnp.mean(jnp.square(y - mean), axis=-1, keepdims=True)
            yn = (y - mean) * lax.rsqrt(var + RW_GN_EPS) * gnw_ref[hh] + gnb_ref[hh]
            bonus = jnp.sum(r * k * rk_ref[hh], axis=-1, keepdims=True) * v
            outs.append((yn + bonus) * g_ref[0, hh, rows, :])
        o_ref[0, rows, :] = jnp.concatenate(outs, axis=-1)


def _rwscan(r, lw, k, v, kk, b, g, r_k, gn_w, gn_b):
    b_, h_, t_, d_ = r.shape
    hp = LANES // d_
    tstep = min(RW_TSTEP, t_)
    seq = pl.BlockSpec((1, hp, tstep, d_), lambda bb, h, i: (bb, h, i, 0))
    par = pl.BlockSpec((hp, 1, d_), lambda bb, h, i: (h, 0, 0))
    return pl.pallas_call(
        _rwscan_kernel,
        out_shape=jax.ShapeDtypeStruct((b_, t_, h_ * d_), F32),
        grid=(b_, h_ // hp, t_ // tstep),
        in_specs=[seq] * 7 + [par] * 3,
        out_specs=pl.BlockSpec((1, tstep, hp * d_), lambda bb, h, i: (bb, i, h)),
        scratch_shapes=[pltpu.VMEM((hp, d_, d_), F32)],
        compiler_params=_params(("parallel", "parallel", "arbitrary")),
        name="rwkv_scan",
    )(r, lw, k, v, kk, b, g, r_k, gn_w, gn_b)


def _attn_kernel(q_ref, k_ref, v_ref, lq1_ref, lk1_ref, lq2_ref, lk2_ref, sw_ref, o_ref,
                 m1_ref, l1_ref, acc1_ref, m2_ref, l2_ref, acc2_ref, *, lam_init):
    qi = pl.program_id(2)
    ki = pl.program_id(3)
    tq, tk = q_ref.shape[1], k_ref.shape[1]

    @pl.when(ki == 0)
    def _():
        for m_ref, l_ref, acc_ref in ((m1_ref, l1_ref, acc1_ref), (m2_ref, l2_ref, acc2_ref)):
            m_ref[...] = jnp.full_like(m_ref, NEG_BIG)
            l_ref[...] = jnp.zeros_like(l_ref)
            acc_ref[...] = jnp.zeros_like(acc_ref)

    @pl.when(ki * tk <= qi * tq + (tq - 1))
    def _():
        q = q_ref[0] * (DA_HD ** -0.5)
        lane = lax.broadcasted_iota(jnp.int32, q.shape, 1)
        kb = k_ref[0].astype(BF16)
        vb = v_ref[0].astype(BF16)
        qpos = qi * tq + lax.broadcasted_iota(jnp.int32, (tq, tk), 0)
        kpos = ki * tk + lax.broadcasted_iota(jnp.int32, (tq, tk), 1)
        causal = kpos <= qpos
        halves = ((lane < DA_HD, m1_ref, l1_ref, acc1_ref), (lane >= DA_HD, m2_ref, l2_ref, acc2_ref))
        for sel, m_ref, l_ref, acc_ref in halves:
            s = _dot_nt(jnp.where(sel, q, 0.0).astype(BF16), kb)
            s = jnp.where(causal, s, NEG_BIG)
            m_old = m_ref[...]
            m_new = jnp.maximum(m_old, jnp.max(s, axis=-1, keepdims=True))
            alpha = jnp.exp(m_old - m_new)
            p = jnp.exp(s - m_new)
            l_ref[...] = alpha * l_ref[...] + jnp.sum(p, axis=-1, keepdims=True)
            acc_ref[...] = alpha * acc_ref[...] + _dot(p.astype(BF16), vb)
            m_ref[...] = m_new

    @pl.when(ki == pl.num_programs(3) - 1)
    def _():
        lam = (jnp.exp(jnp.sum(lq1_ref[...] * lk1_ref[...], axis=-1, keepdims=True))
               - jnp.exp(jnp.sum(lq2_ref[...] * lk2_ref[...], axis=-1, keepdims=True)) + lam_init)
        o = acc1_ref[...] / l1_ref[...] - lam * (acc2_ref[...] / l2_ref[...])
        o = o * lax.rsqrt(jnp.mean(o * o, axis=-1, keepdims=True) + DA_SUBLN_EPS) * sw_ref[...]
        o_ref[0] = o * (1.0 - lam_init)


def _attn(z, lq1, lk1, lq2, lk2, subln_w, lam_init):
    b_, t_, _ = z.shape
    tq, tk = min(ATT_TQ, t_), min(ATT_TK, t_)
    hw = 2 * DA_HD
    q0, k0, v0 = RW_COLS // hw, (RW_COLS + DA_W) // hw, (RW_COLS + 2 * DA_W) // hw

    def kv_map(off):
        def f(b, h, qi, ki):
            return (b, jnp.minimum(ki, (qi * tq + tq - 1) // tk), off + h)
        return f

    small = lambda n: pl.BlockSpec((1, n), lambda b, h, qi, ki: (0, 0))
    return pl.pallas_call(
        functools.partial(_attn_kernel, lam_init=lam_init),
        out_shape=jax.ShapeDtypeStruct((b_, t_, DA_W), F32),
        grid=(b_, DA_HEADS, t_ // tq, t_ // tk),
        in_specs=[
            pl.BlockSpec((1, tq, hw), lambda b, h, qi, ki: (b, qi, q0 + h)),
            pl.BlockSpec((1, tk, hw), kv_map(k0)),
            pl.BlockSpec((1, tk, hw), kv_map(v0)),
            small(DA_HD), small(DA_HD), small(DA_HD), small(DA_HD), small(hw),
        ],
        out_specs=pl.BlockSpec((1, tq, hw), lambda b, h, qi, ki: (b, qi, h)),
        scratch_shapes=[pltpu.VMEM((tq, 1), F32), pltpu.VMEM((tq, 1), F32), pltpu.VMEM((tq, hw), F32),
                        pltpu.VMEM((tq, 1), F32), pltpu.VMEM((tq, 1), F32), pltpu.VMEM((tq, hw), F32)],
        compiler_params=_params(("parallel", "parallel", "parallel", "arbitrary")),
        name="diff_attn",
    )(z, z, z, lq1, lk1, lq2, lk2, subln_w)


def _gelu_tanh(x):
    return 0.5 * x * (1.0 + jnp.tanh(math.sqrt(2.0 / math.pi) * (x + 0.044715 * (x * x * x))))


def _s5_kernel(u_ref, bmat_ref, pw_ref, step_ref, cmat_ref, d_ref, wglu_ref, bglu_ref, o_ref,
               xs_ref, carry_ref):
    @pl.when(pl.program_id(1) == 0)
    def _():
        carry_ref[...] = jnp.zeros_like(carry_ref)

    u = u_ref[0]
    tt = u.shape[0]
    n = SSM_N
    xs_ref[...] = _dot(u.astype(BF16), bmat_ref[...])
    rowi = lax.broadcasted_iota(jnp.int32, (SUBLANES, n), 0)
    pr, pi = pw_ref[0], pw_ref[1]

    def block(i, carry):
        cr, ci = carry
        rows = pl.ds(pl.multiple_of(i * SUBLANES, SUBLANES), SUBLANES)
        xr = xs_ref[rows, :n]
        xi = xs_ref[rows, n:]
        for lvl, d in enumerate((1, 2, 4)):
            ar, ai = step_ref[lvl:lvl + 1, :n], step_ref[lvl:lvl + 1, n:]
            sr = jnp.where(rowi >= d, pltpu.roll(xr, d, 0), 0.0)
            si = jnp.where(rowi >= d, pltpu.roll(xi, d, 0), 0.0)
            xr, xi = xr + ar * sr - ai * si, xi + ar * si + ai * sr
        xr, xi = xr + pr * cr - pi * ci, xi + pr * ci + pi * cr
        xs_ref[rows, :n] = xr
        xs_ref[rows, n:] = xi
        return xr[SUBLANES - 1:, :], xi[SUBLANES - 1:, :]

    cr, ci = lax.fori_loop(0, tt // SUBLANES, block, (carry_ref[0:1, :], carry_ref[1:2, :]))
    carry_ref[0:1, :] = cr
    carry_ref[1:2, :] = ci
    y = _dot(xs_ref[...].astype(BF16), cmat_ref[...]) + d_ref[...] * u
    y = _gelu_tanh(y)
    o_ref[0] = y * jax.nn.sigmoid(_dot(y.astype(BF16), wglu_ref[...]) + bglu_ref[...])


def _s5(z, bmat, pw, step, cmat, d_skip, w_glu, b_glu):
    b_, t_, _ = z.shape
    tt = min(SSM_TSTEP, t_)
    n = SSM_N
    full = lambda shape: pl.BlockSpec(shape, lambda b, i: (0,) * len(shape))
    return pl.pallas_call(
        _s5_kernel,
        out_shape=jax.ShapeDtypeStruct((b_, t_, SSM_W), F32),
        grid=(b_, t_ // tt),
        in_specs=[
            pl.BlockSpec((1, tt, SSM_W), lambda b, i: (b, i, SSM_COL0 // SSM_W)),
            full((SSM_W, 2 * n)), full((2, SUBLANES, n)), full((3, 2 * n)), full((2 * n, SSM_W)),
            full((1, SSM_W)), full((SSM_W, SSM_W)), full((1, SSM_W)),
        ],
        out_specs=pl.BlockSpec((1, tt, SSM_W), lambda b, i: (b, i, 0)),
        scratch_shapes=[pltpu.VMEM((tt, 2 * n), F32), pltpu.VMEM((2, n), F32)],
        compiler_params=_params(("parallel", "arbitrary")),
        name="s5",
    )(z, bmat, pw, step, cmat, d_skip, w_glu, b_glu)


def _s5_discretise(a_re, a_im, log_dt, b_re, b_im, c_re, c_im):
    g_, n_, c_ = SSM_GROUPS, SSM_STATE, SSM_GROUP
    dt = jnp.exp(log_dt)[:, None]
    mag = jnp.exp(dt * a_re)
    abar_r, abar_i = mag * jnp.cos(dt * a_im), mag * jnp.sin(dt * a_im)
    den = a_re * a_re + a_im * a_im
    nr, ni = abar_r - 1.0, abar_i
    coef_r, coef_i = (nr * a_re + ni * a_im) / den, (ni * a_re - nr * a_im) / den
    bbar_r = coef_r[..., None] * b_re - coef_i[..., None] * b_im
    bbar_i = coef_r[..., None] * b_im + coef_i[..., None] * b_re
    eye = jnp.eye(g_, dtype=F32)
    bd_in = lambda m: jnp.einsum('gnc,gh->gchn', m, eye).reshape(g_ * c_, g_ * n_)
    bmat = jnp.concatenate([bd_in(bbar_r), bd_in(bbar_i)], axis=1).astype(BF16)
    bd_out = lambda m: jnp.einsum('gcn,gh->gnhc', m, eye).reshape(g_ * n_, g_ * c_)
    cmat = jnp.concatenate([bd_out(c_re), -bd_out(c_im)], axis=0).astype(BF16)
    ar, ai = abar_r.reshape(1, -1), abar_i.reshape(1, -1)
    pows_r, pows_i = [ar], [ai]
    for _ in range(SUBLANES - 1):
        pr, pi = pows_r[-1], pows_i[-1]
        pows_r.append(pr * ar - pi * ai)
        pows_i.append(pr * ai + pi * ar)
    pw = jnp.stack([jnp.concatenate(pows_r, axis=0), jnp.concatenate(pows_i, axis=0)])
    step = jnp.concatenate([jnp.concatenate([pows_r[d - 1] for d in (1, 2, 4)], axis=0),
                            jnp.concatenate([pows_i[d - 1] for d in (1, 2, 4)], axis=0)], axis=1)
    return bmat, pw, step, cmat


def _outproj_kernel(x_ref, yrw_ref, yda_ref, yss_ref, w1_ref, w2_ref, w3_ref, g_ref, o_ref):
    y = (_dot(yrw_ref[...].astype(BF16), w1_ref[...]) + _dot(yda_ref[...].astype(BF16), w2_ref[...])
         + _dot(yss_ref[...].astype(BF16), w3_ref[...]))
    o_ref[...] = x_ref[...] + _rms(y, g_ref[...], NORM_EPS)


def _outproj(x, y_rw, y_da, y_ss, w1, w2, w3, g):
    m, d = x.shape
    tm = min(ROW_TILE, m)
    rowblk = lambda n: pl.BlockSpec((tm, n), lambda i: (i, 0))
    full = lambda shape: pl.BlockSpec(shape, lambda i: (0, 0))
    return pl.pallas_call(
        _outproj_kernel,
        out_shape=jax.ShapeDtypeStruct((m, d), F32),
        grid=(m // tm,),
        in_specs=[rowblk(d), rowblk(RW_W), rowblk(DA_W), rowblk(SSM_W),
                  full((RW_W, d)), full((DA_W, d)), full((SSM_W, d)), full((1, d))],
        out_specs=rowblk(d),
        compiler_params=_params(("parallel",)),
        name="outproj",
    )(x, y_rw, y_da, y_ss, w1, w2, w3, g)


def _ffn_weights(w_gu, w_down):
    pad = D_FF_PAD - D_FF
    wg = jnp.pad(w_gu[:, :D_FF].astype(BF16), ((0, 0), (0, pad)))
    wu = jnp.pad(w_gu[:, D_FF:].astype(BF16), ((0, 0), (0, pad)))
    wd = jnp.pad(w_down.astype(BF16), ((0, pad), (0, 0)))
    return wg, wu, wd


def kernel(x, ffn1_pre_g, ffn1_w_gu, ffn1_w_down, ffn1_post_g, mix_pre_g, w_in, rw_mu, rw_w0, rw_w2, rw_a0, rw_a2, rw_g2, rw_k_k, rw_k_a, rw_r_k, rw_gn_w, rw_gn_b, da_lq1, da_lk1, da_lq2, da_lk2, da_subln_w, ssm_a_re, ssm_a_im, ssm_log_dt, ssm_b_re, ssm_b_im, ssm_c_re, ssm_c_im, ssm_d, ssm_w_glu, ssm_b_glu, w_out, mix_post_g, ffn2_pre_g, ffn2_w_gu, ffn2_w_down, ffn2_post_g):
    b_, t_, d_ = x.shape
    m = b_ * t_
    row = lambda a: a.reshape(1, -1)
    xf = x.reshape(m, d_)
    for l in range(DEPTH):
        xf = _ffn(xf, row(ffn1_pre_g[l]), *_ffn_weights(ffn1_w_gu[l], ffn1_w_down[l]), row(ffn1_post_g[l]))

        w_in_l = w_in[l].astype(BF16)
        w_in_p = jnp.concatenate(
            [w_in_l[:, :RW_COLS + DA_COLS],
             jnp.zeros((d_, SSM_COL0 - RW_COLS - DA_COLS), BF16),
             w_in_l[:, RW_COLS + DA_COLS:]], axis=1)
        z = _inproj(xf, row(mix_pre_g[l]), w_in_p).reshape(b_, t_, IN_COLS_PAD)

        feats = _rwprep(z, row(rw_mu[l]), row(rw_w0[l]), rw_w2[l], row(rw_a0[l]), rw_a2[l], rw_g2[l],
                        row(rw_k_k[l]), row(rw_k_a[l]))
        per_head = lambda a: a.reshape(RW_HEADS, 1, RW_HD)
        y_rw = _rwscan(*feats, per_head(rw_r_k[l]), per_head(rw_gn_w[l]), per_head(rw_gn_b[l]))

        lam_init = 0.8 - 0.6 * math.exp(-0.3 * l)
        y_da = _attn(z, row(da_lq1[l]), row(da_lk1[l]), row(da_lq2[l]), row(da_lk2[l]),
                     row(da_subln_w[l]), lam_init)

        bmat, pw, step, cmat = _s5_discretise(ssm_a_re[l], ssm_a_im[l], ssm_log_dt[l], ssm_b_re[l],
                                              ssm_b_im[l], ssm_c_re[l], ssm_c_im[l])
        y_ss = _s5(z, bmat, pw, step, cmat, row(ssm_d[l]), ssm_w_glu[l].astype(BF16), row(ssm_b_glu[l]))

        w_out_l = w_out[l].astype(BF16)
        xf = _outproj(xf, y_rw.reshape(m, RW_W), y_da.reshape(m, DA_W), y_ss.reshape(m, SSM_W),
                      w_out_l[:RW_W], w_out_l[RW_W:RW_W + DA_W], w_out_l[RW_W + DA_W:], row(mix_post_g[l]))

        xf = _ffn(xf, row(ffn2_pre_g[l]), *_ffn_weights(ffn2_w_gu[l], ffn2_w_down[l]), row(ffn2_post_g[l]))
    return xf.reshape(b_, t_, d_)
```

```python
import functools
import math

import jax
import jax.numpy as jnp
from jax import lax
from jax.experimental import pallas as pl
from jax.experimental.pallas import tpu as pltpu

F32 = jnp.float32
BF16 = jnp.bfloat16
HIGHEST = lax.Precision.HIGHEST

D_MODEL = 2048
DEPTH = 2
RW_HEADS = 12
RW_HD = 64
RW_W = RW_HEADS * RW_HD
RW_DECAY_LORA = 64
RW_AAA_LORA = 64
RW_GATE_LORA = 128
RW_GN_EPS = 64e-5
DA_HEADS = 6
DA_HD = 64
DA_W = DA_HEADS * 2 * DA_HD
DA_SUBLN_EPS = 1e-5
SSM_W = D_MODEL - RW_W - DA_W
SSM_GROUP = 16
SSM_GROUPS = SSM_W // SSM_GROUP
SSM_STATE = 64
SSM_N = SSM_GROUPS * SSM_STATE
RW_COLS = 3 * RW_W + RW_DECAY_LORA + RW_AAA_LORA + RW_GATE_LORA
DA_COLS = 3 * DA_W
D_FF = 5504
NORM_EPS = 1e-6

LANES = 128
SUBLANES = 8
VMEM_LIMIT = 56 * 1024 * 1024
FF_TILE = 512
D_FF_PAD = -(-D_FF // FF_TILE) * FF_TILE
IN_TILE = 512
SSM_COL0 = -(-(RW_COLS + DA_COLS) // IN_TILE) * IN_TILE
IN_COLS_PAD = SSM_COL0 + SSM_W
ROW_TILE = 512
RW_CHUNK = 64
RW_TSTEP = 256
ATT_TILE = 512
SSM_TSTEP = 256
PREP_TSTEP = 256
NEG_BIG = -1e30


def _dot(a, b, precision=None):
    return jnp.dot(a, b, preferred_element_type=F32, precision=precision)


def _dot_nt(a, b, precision=None):
    return lax.dot_general(a, b, (((1,), (1,)), ((), ())),
                           preferred_element_type=F32, precision=precision)


def _rms(x, g, eps):
    return x * lax.rsqrt(jnp.mean(x * x, axis=-1, keepdims=True) + eps) * g


def _params(sem):
    return pltpu.CompilerParams(dimension_semantics=sem, vmem_limit_bytes=VMEM_LIMIT)


def _ffn_kernel(x_ref, pre_ref, wg_ref, wu_ref, wd_ref, post_ref, o_ref, xn_ref, acc_ref):
    j = pl.program_id(1)

    @pl.when(j == 0)
    def _():
        xn_ref[...] = _rms(x_ref[...], pre_ref[...], NORM_EPS).astype(BF16)
        acc_ref[...] = jnp.zeros_like(acc_ref)

    xn = xn_ref[...]
    gate = _dot(xn, wg_ref[...])
    up = _dot(xn, wu_ref[...])
    h = (gate * jax.nn.sigmoid(gate) * up).astype(BF16)
    acc_ref[...] += _dot(h, wd_ref[...])

    @pl.when(j == pl.num_programs(1) - 1)
    def _():
        o_ref[...] = x_ref[...] + 0.5 * _rms(acc_ref[...], post_ref[...], NORM_EPS)


def _ffn(x, pre_g, wg, wu, wd, post_g):
    m, d = x.shape
    fp = wg.shape[1]
    tm, tf = min(ROW_TILE, m), FF_TILE
    return pl.pallas_call(
        _ffn_kernel,
        out_shape=jax.ShapeDtypeStruct((m, d), F32),
        grid=(m // tm, fp // tf),
        in_specs=[
            pl.BlockSpec((tm, d), lambda i, j: (i, 0)),
            pl.BlockSpec((1, d), lambda i, j: (0, 0)),
            pl.BlockSpec((d, tf), lambda i, j: (0, j)),
            pl.BlockSpec((d, tf), lambda i, j: (0, j)),
            pl.BlockSpec((tf, d), lambda i, j: (j, 0)),
            pl.BlockSpec((1, d), lambda i, j: (0, 0)),
        ],
        out_specs=pl.BlockSpec((tm, d), lambda i, j: (i, 0)),
        scratch_shapes=[pltpu.VMEM((tm, d), BF16), pltpu.VMEM((tm, d), F32)],
        compiler_params=_params(("parallel", "arbitrary")),
        name="ffn",
    )(x, pre_g, wg, wu, wd, post_g)


def _inproj_kernel(x_ref, g_ref, w_ref, o_ref, xn_ref):
    @pl.when(pl.program_id(1) == 0)
    def _():
        xn_ref[...] = _rms(x_ref[...], g_ref[...], NORM_EPS).astype(BF16)

    o_ref[...] = _dot(xn_ref[...], w_ref[...])


def _inproj(x, g, w):
    m, d = x.shape
    n = w.shape[1]
    tm, tn = min(ROW_TILE, m), IN_TILE
    return pl.pallas_call(
        _inproj_kernel,
        out_shape=jax.ShapeDtypeStruct((m, n), F32),
        grid=(m // tm, n // tn),
        in_specs=[
            pl.BlockSpec((tm, d), lambda i, j: (i, 0)),
            pl.BlockSpec((1, d), lambda i, j: (0, 0)),
            pl.BlockSpec((d, tn), lambda i, j: (0, j)),
        ],
        out_specs=pl.BlockSpec((tm, tn), lambda i, j: (i, j)),
        scratch_shapes=[pltpu.VMEM((tm, d), BF16)],
        compiler_params=_params(("parallel", "arbitrary")),
        name="inproj",
    )(x, g, w)


def _softplus(x):
    return jnp.maximum(x, 0.0) + jnp.log(1.0 + jnp.exp(-jnp.abs(x)))


def _rwprep_kernel(z_ref, zp_ref, mu_ref, w0_ref, w2_ref, a0_ref, a2_ref, g2_ref, kk_ref, ka_ref,
                   r_o, lw_o, k_o, v_o, kk_o, b_o, g_o):
    i = pl.program_id(1)
    z = z_ref[0]
    tt = z.shape[0]
    last_prev = jnp.where(i == 0, 0.0, zp_ref[0, SUBLANES - 1:SUBLANES, :])
    row = lax.broadcasted_iota(jnp.int32, z.shape, 0)
    z_prev = jnp.where(row == 0, last_prev, pltpu.roll(z, 1, 0))
    z = z + (z_prev - z) * mu_ref[...]
    o1, o2, o3 = RW_W, 2 * RW_W, 3 * RW_W
    o4, o5 = o3 + RW_DECAY_LORA, o3 + RW_DECAY_LORA + RW_AAA_LORA
    r, k, v = z[:, :o1], z[:, o1:o2], z[:, o2:o3]
    zw, za, zg = z[:, o3:o4], z[:, o4:o5], z[:, o5:]
    w = -_softplus(-(w0_ref[...] + _dot(jnp.tanh(zw), w2_ref[...], HIGHEST))) - 0.5
    lw = -jnp.exp(w)
    a = jax.nn.sigmoid(a0_ref[...] + _dot(za, a2_ref[...], HIGHEST))
    g = _dot(jax.nn.sigmoid(zg), g2_ref[...], HIGHEST)
    kk = k * kk_ref[...]
    k = k * (1.0 + (a - 1.0) * ka_ref[...])
    for h in range(RW_HEADS):
        sl = slice(h * RW_HD, (h + 1) * RW_HD)
        kk_h = kk[:, sl]
        nrm = jnp.sqrt(jnp.sum(kk_h * kk_h, axis=-1, keepdims=True))
        kk_h = kk_h / jnp.maximum(nrm, 1e-12)
        r_o[0, h] = r[:, sl]
        lw_o[0, h] = lw[:, sl]
        k_o[0, h] = k[:, sl]
        v_o[0, h] = v[:, sl]
        kk_o[0, h] = kk_h
        b_o[0, h] = kk_h * a[:, sl]
        g_o[0, h] = g[:, sl]


def _rwprep(z, mu, w0, w2, a0, a2, g2, k_k, k_a):
    b_, t_, _ = z.shape
    tt = min(PREP_TSTEP, t_)
    nprev = tt // SUBLANES
    full = lambda shape: pl.BlockSpec(shape, lambda b, i: (0,) * len(shape))
    head_out = jax.ShapeDtypeStruct((b_, RW_HEADS, t_, RW_HD), F32)
    head_spec = pl.BlockSpec((1, RW_HEADS, tt, RW_HD), lambda b, i: (b, 0, i, 0))
    return pl.pallas_call(
        _rwprep_kernel,
        out_shape=[head_out] * 7,
        grid=(b_, t_ // tt),
        in_specs=[
            pl.BlockSpec((1, tt, RW_COLS), lambda b, i: (b, i, 0)),
            pl.BlockSpec((1, SUBLANES, RW_COLS), lambda b, i: (b, jnp.maximum(i * nprev - 1, 0), 0)),
            full((1, RW_COLS)), full((1, RW_W)), full((RW_DECAY_LORA, RW_W)), full((1, RW_W)),
            full((RW_AAA_LORA, RW_W)), full((RW_GATE_LORA, RW_W)), full((1, RW_W)), full((1, RW_W)),
        ],
        out_specs=[head_spec] * 7,
        compiler_params=_params(("parallel", "arbitrary")),
        name="rwkv_features",
    )(z, z, mu, w0, w2, a0, a2, g2, k_k, k_a)


_NN = ((1,), (0,))
_NT = ((1,), (1,))
_TN = ((0,), (0,))


def _bdot(a, b, dims=_NN):
    return lax.dot_general(a.astype(BF16), b.astype(BF16), (dims, ((), ())), preferred_element_type=F32)


def _unit_lower_inverses(ns, row, col):
    c = ns[0].shape[0]
    eye = (row == col).astype(F32)
    base = SUBLANES
    diag = (row // base) == (col // base)
    a0 = [jnp.where(diag, n, 0.0) for n in ns]
    a2 = [_bdot(a, a) for a in a0]
    a4 = [_bdot(a, a) for a in a2]
    xs = [_bdot(eye + p, eye + q) for p, q in zip(a0, a2)]
    xs = [_bdot(p, eye + q) for p, q in zip(xs, a4)]
    m = base
    while m < c:
        off = ((row // (2 * m)) == (col // (2 * m))) & ((row // m) != (col // m))
        xe = [_bdot(p, jnp.where(off, n, 0.0)) for p, n in zip(xs, ns)]
        xs = [p + _bdot(q, p) for p, q in zip(xs, xe)]
        m *= 2
    return xs


def _rwchunk_kernel(r_ref, lw_ref, k_ref, v_ref, kk_ref, b_ref, rk_ref, rp_o, y0_o, tm_o, sadd_o, bonus_o):
    c = RW_CHUNK
    nheads, tstep, kd = r_ref.shape[1:]
    row = lax.broadcasted_iota(jnp.int32, (c, c), 0)
    col = lax.broadcasted_iota(jnp.int32, (c, c), 1)
    strict = row > col
    incl = row >= col
    tri = incl.astype(BF16)
    items = [(hh, pl.ds(ci * c, c)) for hh in range(nheads) for ci in range(tstep // c)]
    ld = lambda ref: [ref[0, hh, rows, :] for hh, rows in items]
    r, lw, k, v, kk, b = ld(r_ref), ld(lw_ref), ld(k_ref), ld(v_ref), ld(kk_ref), ld(b_ref)

    lw_hi = [x.astype(BF16) for x in lw]
    lc_hi = [_bdot(tri, h) for h in lw_hi]
    lc_lo = [_bdot(tri, x - h.astype(F32)) for x, h in zip(lw, lw_hi)]
    lc = [p + q for p, q in zip(lc_hi, lc_lo)]
    e_inc = [jnp.exp(x) for x in lc]
    e_inv = [jnp.exp(-x) for x in lc]
    e_end = [jnp.exp(x[c - 1:c, :] - x) for x in lc]
    at = [-q * jnp.exp(x - w) for q, x, w in zip(kk, lc, lw)]
    rt = [p * e for p, e in zip(r, e_inc)]
    m1 = [_bdot(jnp.concatenate([p, q], axis=0), jnp.concatenate([bb * e, kx * e], axis=0), _NT)
          for p, q, bb, kx, e in zip(at, rt, b, k, e_inv)]
    a_ab = [jnp.where(strict, m[:c, :c], 0.0) for m in m1]
    a_ak = [jnp.where(strict, m[:c, c:], 0.0) for m in m1]
    r_b = [jnp.where(incl, m[c:, :c], 0.0) for m in m1]
    r_k = [jnp.where(incl, m[c:, c:], 0.0) for m in m1]
    av = [_bdot(p, q) for p, q in zip(a_ak, v)]
    rkv = [_bdot(p, q) for p, q in zip(r_k, v)]
    kev = [_bdot(kx * e, q, _TN) for kx, e, q in zip(k, e_end, v)]
    xs = _unit_lower_inverses(a_ab, row, col)
    z = [_bdot(x, jnp.concatenate([p, q], axis=1)) for x, p, q in zip(xs, at, av)]
    rw = [_bdot(p, q) for p, q in zip(r_b, z)]
    tz = [_bdot(bb * e, q, _TN) for bb, e, q in zip(b, e_end, z)]
    eye_k = lax.broadcasted_iota(jnp.int32, (kd, kd), 0) == lax.broadcasted_iota(jnp.int32, (kd, kd), 1)
    for i, (hh, rows) in enumerate(items):
        rp_o[0, hh, rows, :] = rt[i] + rw[i][:, :kd]
        y0_o[0, hh, rows, :] = rw[i][:, kd:] + rkv[i]
        tm_o[0, hh, rows, :] = jnp.where(eye_k, jnp.broadcast_to(e_inc[i][c - 1:c, :], (kd, kd)), 0.0) + tz[i][:, :kd]
        sadd_o[0, hh, rows, :] = tz[i][:, kd:] + kev[i]
        bonus_o[0, hh, rows, :] = jnp.sum(r[i] * k[i] * rk_ref[hh], axis=-1, keepdims=True) * v[i]


def _rwchunk(r, lw, k, v, kk, b, r_k):
    b_, h_, t_, d_ = r.shape
    assert RW_CHUNK == d_
    hp = 2
    tstep = min(RW_TSTEP, t_)
    seq = pl.BlockSpec((1, hp, tstep, d_), lambda bb, h, i: (bb, h, i, 0))
    par = pl.BlockSpec((hp, 1, d_), lambda bb, h, i: (h, 0, 0))
    return pl.pallas_call(
        _rwchunk_kernel,
        out_shape=[jax.ShapeDtypeStruct(r.shape, F32)] * 5,
        grid=(b_, h_ // hp, t_ // tstep),
        in_specs=[seq] * 6 + [par],
        out_specs=[seq] * 5,
        compiler_params=_params(("parallel", "parallel", "parallel")),
        name="rwkv_chunk",
    )(r, lw, k, v, kk, b, r_k)


def _rwstate_kernel(rp_ref, y0_ref, tm_ref, sadd_ref, bonus_ref, g_ref, gnw_ref, gnb_ref, o_ref, s_ref):
    @pl.when(pl.program_id(1) == 0)
    def _():
        s_ref[...] = jnp.zeros_like(s_ref)

    c = RW_CHUNK
    nheads, tstep, _ = rp_ref.shape[1:]
    heads = range(nheads)
    for ci in range(tstep // c):
        rows = pl.ds(ci * c, c)
        st = [s_ref[hh] for hh in heads]
        new = [_bdot(tm_ref[0, hh, rows, :], st[hh]) for hh in heads]
        ys = [_bdot(rp_ref[0, hh, rows, :], st[hh]) for hh in heads]
        outs = []
        for hh in heads:
            s_ref[hh] = new[hh] + sadd_ref[0, hh, rows, :]
            y = ys[hh] + y0_ref[0, hh, rows, :]
            mean = jnp.mean(y, axis=-1, keepdims=True)
            var = jnp.mean(jnp.square(y - mean), axis=-1, keepdims=True)
            yn = (y - mean) * lax.rsqrt(var + RW_GN_EPS) * gnw_ref[hh] + gnb_ref[hh]
            outs.append((yn + bonus_ref[0, hh, rows, :]) * g_ref[0, hh, rows, :])
        o_ref[0, rows, :] = jnp.concatenate(outs, axis=-1)


def _rwstate(rp, y0, tm, sadd, bonus, g, gn_w, gn_b):
    b_, h_, t_, d_ = rp.shape
    tstep = min(RW_TSTEP, t_)
    seq = pl.BlockSpec((1, h_, tstep, d_), lambda bb, i: (bb, 0, i, 0))
    par = pl.BlockSpec((h_, 1, d_), lambda bb, i: (0, 0, 0))
    return pl.pallas_call(
        _rwstate_kernel,
        out_shape=jax.ShapeDtypeStruct((b_, t_, h_ * d_), F32),
        grid=(b_, t_ // tstep),
        in_specs=[seq] * 6 + [par] * 2,
        out_specs=pl.BlockSpec((1, tstep, h_ * d_), lambda bb, i: (bb, i, 0)),
        scratch_shapes=[pltpu.VMEM((h_, d_, d_), F32)],
        compiler_params=_params(("parallel", "arbitrary")),
        name="rwkv_state",
    )(rp, y0, tm, sadd, bonus, g, gn_w, gn_b)


def _attn_kernel(q_ref, k_ref, v_ref, lq1_ref, lk1_ref, lq2_ref, lk2_ref, sw_ref, o_ref,
                 qs_ref, m_ref, acc_ref, *, lam_init):
    qi = pl.program_id(2)
    t = q_ref.shape[1]
    q = q_ref[0] * (DA_HD ** -0.5)
    lane = lax.broadcasted_iota(jnp.int32, q.shape, 1)
    qs_ref[0] = jnp.where(lane < DA_HD, q, 0.0).astype(BF16)
    qs_ref[1] = jnp.where(lane >= DA_HD, q, 0.0).astype(BF16)
    m_ref[...] = jnp.full_like(m_ref, NEG_BIG)
    acc_ref[...] = jnp.zeros_like(acc_ref)
    hw = q.shape[1]
    ones = jnp.ones((t, hw), BF16)
    subheads = range(2)

    def key_tile(j, diagonal):
        rows = pl.ds(pl.multiple_of(j * t, t), t)
        kb = k_ref[0, rows, :].astype(BF16)
        vb = jnp.concatenate([v_ref[0, rows, :].astype(BF16), ones], axis=1)
        s = [_dot_nt(qs_ref[c], kb) for c in subheads]
        if diagonal:
            causal = (lax.broadcasted_iota(jnp.int32, (t, t), 1) <= lax.broadcasted_iota(jnp.int32, (t, t), 0))
            s = [jnp.where(causal, x, NEG_BIG) for x in s]
        m_old = [m_ref[c] for c in subheads]
        m_new = [jnp.maximum(m, jnp.max(x, axis=-1, keepdims=True)) for m, x in zip(m_old, s)]
        p = [jnp.exp(x - jnp.concatenate([m] * (t // hw), axis=1)).astype(BF16) for x, m in zip(s, m_new)]
        pv = [_dot(x, vb) for x in p]
        for c in subheads:
            alpha = jnp.exp(m_old[c] - m_new[c])
            acc_ref[c] = jnp.concatenate([alpha, alpha], axis=1) * acc_ref[c] + pv[c]
            m_ref[c] = m_new[c]

    def below_diagonal(j, carry):
        key_tile(j, False)
        return carry

    lax.fori_loop(0, qi, below_diagonal, 0)
    key_tile(qi, True)

    lam = (jnp.exp(jnp.sum(lq1_ref[...] * lk1_ref[...], axis=-1, keepdims=True))
           - jnp.exp(jnp.sum(lq2_ref[...] * lk2_ref[...], axis=-1, keepdims=True)) + lam_init)
    o = acc_ref[0, :, :hw] / acc_ref[0, :, hw:] - lam * (acc_ref[1, :, :hw] / acc_ref[1, :, hw:])
    o = o * lax.rsqrt(jnp.mean(o * o, axis=-1, keepdims=True) + DA_SUBLN_EPS) * sw_ref[...]
    o_ref[0] = o * (1.0 - lam_init)


def _attn(z, lq1, lk1, lq2, lk2, subln_w, lam_init):
    b_, t_, _ = z.shape
    tq = min(ATT_TILE, t_)
    hw = 2 * DA_HD
    q0, k0, v0 = RW_COLS // hw, (RW_COLS + DA_W) // hw, (RW_COLS + 2 * DA_W) // hw
    small = lambda n: pl.BlockSpec((1, n), lambda b, h, qi: (0, 0))
    return pl.pallas_call(
        functools.partial(_attn_kernel, lam_init=lam_init),
        out_shape=jax.ShapeDtypeStruct((b_, t_, DA_W), F32),
        grid=(b_, DA_HEADS, t_ // tq),
        in_specs=[
            pl.BlockSpec((1, tq, hw), lambda b, h, qi: (b, qi, q0 + h)),
            pl.BlockSpec((1, t_, hw), lambda b, h, qi: (b, 0, k0 + h)),
            pl.BlockSpec((1, t_, hw), lambda b, h, qi: (b, 0, v0 + h)),
            small(DA_HD), small(DA_HD), small(DA_HD), small(DA_HD), small(hw),
        ],
        out_specs=pl.BlockSpec((1, tq, hw), lambda b, h, qi: (b, qi, h)),
        scratch_shapes=[pltpu.VMEM((2, tq, hw), BF16), pltpu.VMEM((2, tq, hw), F32),
                        pltpu.VMEM((2, tq, 2 * hw), F32)],
        compiler_params=_params(("parallel", "parallel", "arbitrary")),
        name="diff_attn",
    )(z, z, z, lq1, lk1, lq2, lk2, subln_w)


def _gelu_tanh(x):
    return 0.5 * x * (1.0 + jnp.tanh(math.sqrt(2.0 / math.pi) * (x + 0.044715 * (x * x * x))))


def _s5_kernel(u_ref, bmat_ref, pw_ref, step_ref, cmat_ref, d_ref, wglu_ref, bglu_ref, o_ref,
               xs_ref, carry_ref):
    @pl.when(pl.program_id(1) == 0)
    def _():
        carry_ref[...] = jnp.zeros_like(carry_ref)

    u = u_ref[0]
    tt = u.shape[0]
    n = SSM_N
    xs_ref[...] = _dot(u.astype(BF16), bmat_ref[...])
    rowi = lax.broadcasted_iota(jnp.int32, (SUBLANES, n), 0)
    pr, pi = pw_ref[0], pw_ref[1]

    def block(i, carry):
        cr, ci = carry
        rows = pl.ds(pl.multiple_of(i * SUBLANES, SUBLANES), SUBLANES)
        xr = xs_ref[rows, :n]
        xi = xs_ref[rows, n:]
        for lvl, d in enumerate((1, 2, 4)):
            ar, ai = step_ref[lvl:lvl + 1, :n], step_ref[lvl:lvl + 1, n:]
            sr = jnp.where(rowi >= d, pltpu.roll(xr, d, 0), 0.0)
            si = jnp.where(rowi >= d, pltpu.roll(xi, d, 0), 0.0)
            xr, xi = xr + ar * sr - ai * si, xi + ar * si + ai * sr
        xr, xi = xr + pr * cr - pi * ci, xi + pr * ci + pi * cr
        xs_ref[rows, :n] = xr
        xs_ref[rows, n:] = xi
        return xr[SUBLANES - 1:, :], xi[SUBLANES - 1:, :]

    cr, ci = lax.fori_loop(0, tt // SUBLANES, block, (carry_ref[0:1, :], carry_ref[1:2, :]))
    carry_ref[0:1, :] = cr
    carry_ref[1:2, :] = ci
    y = _dot(xs_ref[...].astype(BF16), cmat_ref[...]) + d_ref[...] * u
    y = _gelu_tanh(y)
    o_ref[0] = y * jax.nn.sigmoid(_dot(y.astype(BF16), wglu_ref[...]) + bglu_ref[...])


def _s5(z, bmat, pw, step, cmat, d_skip, w_glu, b_glu):
    b_, t_, _ = z.shape
    tt = min(SSM_TSTEP, t_)
    n = SSM_N
    full = lambda shape: pl.BlockSpec(shape, lambda b, i: (0,) * len(shape))
    return pl.pallas_call(
        _s5_kernel,
        out_shape=jax.ShapeDtypeStruct((b_, t_, SSM_W), F32),
        grid=(b_, t_ // tt),
        in_specs=[
            pl.BlockSpec((1, tt, SSM_W), lambda b, i: (b, i, SSM_COL0 // SSM_W)),
            full((SSM_W, 2 * n)), full((2, SUBLANES, n)), full((3, 2 * n)), full((2 * n, SSM_W)),
            full((1, SSM_W)), full((SSM_W, SSM_W)), full((1, SSM_W)),
        ],
        out_specs=pl.BlockSpec((1, tt, SSM_W), lambda b, i: (b, i, 0)),
        scratch_shapes=[pltpu.VMEM((tt, 2 * n), F32), pltpu.VMEM((2, n), F32)],
        compiler_params=_params(("parallel", "arbitrary")),
        name="s5",
    )(z, bmat, pw, step, cmat, d_skip, w_glu, b_glu)


def _s5_discretise(a_re, a_im, log_dt, b_re, b_im, c_re, c_im):
    g_, n_, c_ = SSM_GROUPS, SSM_STATE, SSM_GROUP
    dt = jnp.exp(log_dt)[:, None]
    mag = jnp.exp(dt * a_re)
    abar_r, abar_i = mag * jnp.cos(dt * a_im), mag * jnp.sin(dt * a_im)
    den = a_re * a_re + a_im * a_im
    nr, ni = abar_r - 1.0, abar_i
    coef_r, coef_i = (nr * a_re + ni * a_im) / den, (ni * a_re - nr * a_im) / den
    bbar_r = coef_r[..., None] * b_re - coef_i[..., None] * b_im
    bbar_i = coef_r[..., None] * b_im + coef_i[..., None] * b_re
    eye = jnp.eye(g_, dtype=F32)
    bd_in = lambda m: jnp.einsum('gnc,gh->gchn', m, eye).reshape(g_ * c_, g_ * n_)
    bmat = jnp.concatenate([bd_in(bbar_r), bd_in(bbar_i)], axis=1).astype(BF16)
    bd_out = lambda m: jnp.einsum('gcn,gh->gnhc', m, eye).reshape(g_ * n_, g_ * c_)
    cmat = jnp.concatenate([bd_out(c_re), -bd_out(c_im)], axis=0).astype(BF16)
    ar, ai = abar_r.reshape(1, -1), abar_i.reshape(1, -1)
    pows_r, pows_i = [ar], [ai]
    for _ in range(SUBLANES - 1):
        pr, pi = pows_r[-1], pows_i[-1]
        pows_r.append(pr * ar - pi * ai)
        pows_i.append(pr * ai + pi * ar)
    pw = jnp.stack([jnp.concatenate(pows_r, axis=0), jnp.concatenate(pows_i, axis=0)])
    step = jnp.concatenate([jnp.concatenate([pows_r[d - 1] for d in (1, 2, 4)], axis=0),
                            jnp.concatenate([pows_i[d - 1] for d in (1, 2, 4)], axis=0)], axis=1)
    return bmat, pw, step, cmat


def _outproj_kernel(x_ref, yrw_ref, yda_ref, yss_ref, w1_ref, w2_ref, w3_ref, g_ref, o_ref):
    y = (_dot(yrw_ref[...].astype(BF16), w1_ref[...]) + _dot(yda_ref[...].astype(BF16), w2_ref[...])
         + _dot(yss_ref[...].astype(BF16), w3_ref[...]))
    o_ref[...] = x_ref[...] + _rms(y, g_ref[...], NORM_EPS)


def _outproj(x, y_rw, y_da, y_ss, w1, w2, w3, g):
    m, d = x.shape
    tm = min(ROW_TILE, m)
    rowblk = lambda n: pl.BlockSpec((tm, n), lambda i: (i, 0))
    full = lambda shape: pl.BlockSpec(shape, lambda i: (0, 0))
    return pl.pallas_call(
        _outproj_kernel,
        out_shape=jax.ShapeDtypeStruct((m, d), F32),
        grid=(m // tm,),
        in_specs=[rowblk(d), rowblk(RW_W), rowblk(DA_W), rowblk(SSM_W),
                  full((RW_W, d)), full((DA_W, d)), full((SSM_W, d)), full((1, d))],
        out_specs=rowblk(d),
        compiler_params=_params(("parallel",)),
        name="outproj",
    )(x, y_rw, y_da, y_ss, w1, w2, w3, g)


def _ffn_weights(w_gu, w_down):
    pad = D_FF_PAD - D_FF
    wg = jnp.pad(w_gu[:, :D_FF].astype(BF16), ((0, 0), (0, pad)))
    wu = jnp.pad(w_gu[:, D_FF:].astype(BF16), ((0, 0), (0, pad)))
    wd = jnp.pad(w_down.astype(BF16), ((0, pad), (0, 0)))
    return wg, wu, wd


def kernel(x, ffn1_pre_g, ffn1_w_gu, ffn1_w_down, ffn1_post_g, mix_pre_g, w_in, rw_mu, rw_w0, rw_w2, rw_a0, rw_a2, rw_g2, rw_k_k, rw_k_a, rw_r_k, rw_gn_w, rw_gn_b, da_lq1, da_lk1, da_lq2, da_lk2, da_subln_w, ssm_a_re, ssm_a_im, ssm_log_dt, ssm_b_re, ssm_b_im, ssm_c_re, ssm_c_im, ssm_d, ssm_w_glu, ssm_b_glu, w_out, mix_post_g, ffn2_pre_g, ffn2_w_gu, ffn2_w_down, ffn2_post_g):
    b_, t_, d_ = x.shape
    m = b_ * t_
    row = lambda a: a.reshape(1, -1)
    xf = x.reshape(m, d_)
    for l in range(DEPTH):
        xf = _ffn(xf, row(ffn1_pre_g[l]), *_ffn_weights(ffn1_w_gu[l], ffn1_w_down[l]), row(ffn1_post_g[l]))

        w_in_l = w_in[l].astype(BF16)
        w_in_p = jnp.concatenate(
            [w_in_l[:, :RW_COLS + DA_COLS],
             jnp.zeros((d_, SSM_COL0 - RW_COLS - DA_COLS), BF16),
             w_in_l[:, RW_COLS + DA_COLS:]], axis=1)
        z = _inproj(xf, row(mix_pre_g[l]), w_in_p).reshape(b_, t_, IN_COLS_PAD)

        feats = _rwprep(z, row(rw_mu[l]), row(rw_w0[l]), rw_w2[l], row(rw_a0[l]), rw_a2[l], rw_g2[l],
                        row(rw_k_k[l]), row(rw_k_a[l]))
        per_head = lambda a: a.reshape(RW_HEADS, 1, RW_HD)
        r, lw, k, v, kk, b, g = feats
        maps = _rwchunk(r, lw, k, v, kk, b, per_head(rw_r_k[l]))
        y_rw = _rwstate(*maps, g, per_head(rw_gn_w[l]), per_head(rw_gn_b[l]))

        lam_init = 0.8 - 0.6 * math.exp(-0.3 * l)
        y_da = _attn(z, row(da_lq1[l]), row(da_lk1[l]), row(da_lq2[l]), row(da_lk2[l]),
                     row(da_subln_w[l]), lam_init)

        bmat, pw, step, cmat = _s5_discretise(ssm_a_re[l], ssm_a_im[l], ssm_log_dt[l], ssm_b_re[l],
                                              ssm_b_im[l], ssm_c_re[l], ssm_c_im[l])
        y_ss = _s5(z, bmat, pw, step, cmat, row(ssm_d[l]), ssm_w_glu[l].astype(BF16), row(ssm_b_glu[l]))

        w_out_l = w_out[l].astype(BF16)
        xf = _outproj(xf, y_rw.reshape(m, RW_W), y_da.reshape(m, DA_W), y_ss.reshape(m, SSM_W),
                      w_out_l[:RW_W], w_out_l[RW_W:RW_W + DA_W], w_out_l[RW_W + DA_W:], row(mix_post_g[l]))

        xf = _ffn(xf, row(ffn2_pre_g[l]), *_ffn_weights(ffn2_w_gu[l], ffn2_w_down[l]), row(ffn2_post_g[l]))
    return xf.reshape(b_, t_, d_)
```

```python
import functools
import math

import jax
import jax.numpy as jnp
from jax import lax
from jax.experimental import pallas as pl
from jax.experimental.pallas import tpu as pltpu

F32 = jnp.float32
BF16 = jnp.bfloat16
HIGHEST = lax.Precision.HIGHEST

D_MODEL = 2048
DEPTH = 2
RW_HEADS = 12
RW_HD = 64
RW_W = RW_HEADS * RW_HD
RW_DECAY_LORA = 64
RW_AAA_LORA = 64
RW_GATE_LORA = 128
RW_GN_EPS = 64e-5
DA_HEADS = 6
DA_HD = 64
DA_W = DA_HEADS * 2 * DA_HD
DA_SUBLN_EPS = 1e-5
SSM_W = D_MODEL - RW_W - DA_W
SSM_GROUP = 16
SSM_GROUPS = SSM_W // SSM_GROUP
SSM_STATE = 64
SSM_N = SSM_GROUPS * SSM_STATE
RW_COLS = 3 * RW_W + RW_DECAY_LORA + RW_AAA_LORA + RW_GATE_LORA
DA_COLS = 3 * DA_W
D_FF = 5504
NORM_EPS = 1e-6

LANES = 128
SUBLANES = 8
VMEM_LIMIT = 56 * 1024 * 1024
FF_TILE = 256
FFN_ROW_TILE = 1024
IN_ROW_TILE = 1024
IN_TILE = 512
SSM_COL0 = -(-(RW_COLS + DA_COLS) // IN_TILE) * IN_TILE
IN_COLS_PAD = SSM_COL0 + SSM_W
ROW_TILE = 512
RW_CHUNK = 64
RW_TSTEP = 256
RW_MAP_HEADS = 4
RW_MAP_TSTEP = 512
ATT_TILE = 512
SSM_TSTEP = 256
PREP_TSTEP = 256
NEG_BIG = -1e30


def _dot(a, b, precision=None):
    return jnp.dot(a, b, preferred_element_type=F32, precision=precision)


def _dot_nt(a, b, precision=None):
    return lax.dot_general(a, b, (((1,), (1,)), ((), ())),
                           preferred_element_type=F32, precision=precision)


def _rms(x, g, eps):
    return x * lax.rsqrt(jnp.mean(x * x, axis=-1, keepdims=True) + eps) * g


def _params(sem):
    return pltpu.CompilerParams(dimension_semantics=sem, vmem_limit_bytes=VMEM_LIMIT)


def _ff_offset(j, base=0):
    return (base // LANES + jnp.minimum(j * (FF_TILE // LANES), (D_FF - FF_TILE) // LANES)) * LANES


def _ffn_kernel(x_ref, pre_ref, wg_ref, wu_ref, wd_ref, post_ref, o_ref, xn_ref, acc_ref):
    j = pl.program_id(1)

    @pl.when(j == 0)
    def _():
        xn_ref[...] = _rms(x_ref[...], pre_ref[...], NORM_EPS).astype(BF16)
        acc_ref[...] = jnp.zeros_like(acc_ref)

    xn = xn_ref[...]
    gate = _dot(xn, wg_ref[0].astype(BF16))
    up = _dot(xn, wu_ref[0].astype(BF16))
    h = gate * jax.nn.sigmoid(gate) * up
    unit = _ff_offset(j) + lax.broadcasted_iota(jnp.int32, h.shape, 1)
    h = jnp.where(unit >= j * FF_TILE, h, 0.0).astype(BF16)
    acc_ref[...] += _dot(h, wd_ref[0].astype(BF16))

    @pl.when(j == pl.num_programs(1) - 1)
    def _():
        o_ref[...] = x_ref[...] + 0.5 * _rms(acc_ref[...], post_ref[...], NORM_EPS)


def _ffn(x, pre_g, w_gu, w_down, post_g, layer):
    m, d = x.shape
    tm, tf = min(FFN_ROW_TILE, m), FF_TILE
    once = pl.Buffered(1)
    return pl.pallas_call(
        _ffn_kernel,
        out_shape=jax.ShapeDtypeStruct((m, d), F32),
        grid=(m // tm, pl.cdiv(D_FF, tf)),
        in_specs=[
            pl.BlockSpec((tm, d), lambda i, j: (i, 0), pipeline_mode=once),
            pl.BlockSpec((1, d), lambda i, j: (0, 0)),
            pl.BlockSpec((pl.Element(1), pl.Element(d), pl.Element(tf)), lambda i, j: (layer, 0, _ff_offset(j))),
            pl.BlockSpec((pl.Element(1), pl.Element(d), pl.Element(tf)), lambda i, j: (layer, 0, _ff_offset(j, D_FF))),
            pl.BlockSpec((pl.Element(1), pl.Element(tf), pl.Element(d)), lambda i, j: (layer, _ff_offset(j), 0)),
            pl.BlockSpec((1, d), lambda i, j: (0, 0)),
        ],
        out_specs=pl.BlockSpec((tm, d), lambda i, j: (i, 0), pipeline_mode=once),
        scratch_shapes=[pltpu.VMEM((tm, d), BF16), pltpu.VMEM((tm, d), F32)],
        compiler_params=_params(("parallel", "arbitrary")),
        name="ffn",
    )(x, pre_g, w_gu, w_gu, w_down, post_g)


def _inproj_kernel(x_ref, g_ref, w_ref, o_ref, xn_ref):
    @pl.when(pl.program_id(1) == 0)
    def _():
        xn_ref[...] = _rms(x_ref[...], g_ref[...], NORM_EPS).astype(BF16)

    o_ref[...] = _dot(xn_ref[...], w_ref[0].astype(BF16))


def _inproj(x, g, w_in, layer):
    m, d = x.shape
    tm, tn = min(IN_ROW_TILE, m), IN_TILE
    gap = SSM_COL0 - (RW_COLS + DA_COLS)

    def w_col(j):
        return (j * (tn // LANES) - jnp.where(j * tn >= SSM_COL0, gap // LANES, 0)) * LANES

    return pl.pallas_call(
        _inproj_kernel,
        out_shape=jax.ShapeDtypeStruct((m, IN_COLS_PAD), F32),
        grid=(m // tm, IN_COLS_PAD // tn),
        in_specs=[
            pl.BlockSpec((tm, d), lambda i, j: (i, 0)),
            pl.BlockSpec((1, d), lambda i, j: (0, 0)),
            pl.BlockSpec((pl.Element(1), pl.Element(d), pl.Element(tn)), lambda i, j: (layer, 0, w_col(j))),
        ],
        out_specs=pl.BlockSpec((tm, tn), lambda i, j: (i, j)),
        scratch_shapes=[pltpu.VMEM((tm, d), BF16)],
        compiler_params=_params(("parallel", "arbitrary")),
        name="inproj",
    )(x, g, w_in)


def _softplus(x):
    return jnp.maximum(x, 0.0) + jnp.log(1.0 + jnp.exp(-jnp.abs(x)))


def _rwprep_kernel(z_ref, zp_ref, mu_ref, w0_ref, w2_ref, a0_ref, a2_ref, g2_ref, kk_ref, ka_ref,
                   r_o, lw_o, k_o, v_o, kk_o, b_o, g_o):
    i = pl.program_id(1)
    z = z_ref[0]
    tt = z.shape[0]
    last_prev = jnp.where(i == 0, 0.0, zp_ref[0, SUBLANES - 1:SUBLANES, :])
    row = lax.broadcasted_iota(jnp.int32, z.shape, 0)
    z_prev = jnp.where(row == 0, last_prev, pltpu.roll(z, 1, 0))
    z = z + (z_prev - z) * mu_ref[...]
    o1, o2, o3 = RW_W, 2 * RW_W, 3 * RW_W
    o4, o5 = o3 + RW_DECAY_LORA, o3 + RW_DECAY_LORA + RW_AAA_LORA
    r, k, v = z[:, :o1], z[:, o1:o2], z[:, o2:o3]
    zw, za, zg = z[:, o3:o4], z[:, o4:o5], z[:, o5:]
    w = -_softplus(-(w0_ref[...] + _dot(jnp.tanh(zw), w2_ref[...], HIGHEST))) - 0.5
    lw = -jnp.exp(w)
    a = jax.nn.sigmoid(a0_ref[...] + _dot(za, a2_ref[...], HIGHEST))
    g = _dot(jax.nn.sigmoid(zg), g2_ref[...], HIGHEST)
    kk = k * kk_ref[...]
    k = k * (1.0 + (a - 1.0) * ka_ref[...])
    for h in range(RW_HEADS):
        sl = slice(h * RW_HD, (h + 1) * RW_HD)
        kk_h = kk[:, sl]
        nrm = jnp.sqrt(jnp.sum(kk_h * kk_h, axis=-1, keepdims=True))
        kk_h = kk_h / jnp.maximum(nrm, 1e-12)
        r_o[0, h] = r[:, sl]
        lw_o[0, h] = lw[:, sl]
        k_o[0, h] = k[:, sl]
        v_o[0, h] = v[:, sl]
        kk_o[0, h] = kk_h
        b_o[0, h] = kk_h * a[:, sl]
        g_o[0, h] = g[:, sl]


def _rwprep(z, mu, w0, w2, a0, a2, g2, k_k, k_a):
    b_, t_, _ = z.shape
    tt = min(PREP_TSTEP, t_)
    nprev = tt // SUBLANES
    full = lambda shape: pl.BlockSpec(shape, lambda b, i: (0,) * len(shape))
    head_out = jax.ShapeDtypeStruct((b_, RW_HEADS, t_, RW_HD), F32)
    head_spec = pl.BlockSpec((1, RW_HEADS, tt, RW_HD), lambda b, i: (b, 0, i, 0))
    return pl.pallas_call(
        _rwprep_kernel,
        out_shape=[head_out] * 7,
        grid=(b_, t_ // tt),
        in_specs=[
            pl.BlockSpec((1, tt, RW_COLS), lambda b, i: (b, i, 0)),
            pl.BlockSpec((1, SUBLANES, RW_COLS), lambda b, i: (b, jnp.maximum(i * nprev - 1, 0), 0)),
            full((1, RW_COLS)), full((1, RW_W)), full((RW_DECAY_LORA, RW_W)), full((1, RW_W)),
            full((RW_AAA_LORA, RW_W)), full((RW_GATE_LORA, RW_W)), full((1, RW_W)), full((1, RW_W)),
        ],
        out_specs=[head_spec] * 7,
        compiler_params=_params(("parallel", "arbitrary")),
        name="rwkv_features",
    )(z, z, mu, w0, w2, a0, a2, g2, k_k, k_a)


_NN = ((1,), (0,))
_NT = ((1,), (1,))
_TN = ((0,), (0,))


def _bdot(a, b, dims=_NN):
    return lax.dot_general(a.astype(BF16), b.astype(BF16), (dims, ((), ())), preferred_element_type=F32)


def _unit_lower_inverses(ns, row, col):
    c = ns[0].shape[0]
    eye = (row == col).astype(F32)
    base = SUBLANES
    diag = (row // base) == (col // base)
    a0 = [jnp.where(diag, n, 0.0) for n in ns]
    a2 = [_bdot(a, a) for a in a0]
    a4 = [_bdot(a, a) for a in a2]
    xs = [_bdot(eye + p, eye + q) for p, q in zip(a0, a2)]
    xs = [_bdot(p, eye + q) for p, q in zip(xs, a4)]
    m = base
    while m < c:
        off = ((row // (2 * m)) == (col // (2 * m))) & ((row // m) != (col // m))
        xe = [_bdot(p, jnp.where(off, n, 0.0)) for p, n in zip(xs, ns)]
        xs = [p + _bdot(q, p) for p, q in zip(xs, xe)]
        m *= 2
    return xs


def _rwchunk_kernel(r_ref, lw_ref, k_ref, v_ref, kk_ref, b_ref, rk_ref, rp_o, y0_o, tm_o, sadd_o, bonus_o):
    c = RW_CHUNK
    nheads, tstep, kd = r_ref.shape[1:]
    row = lax.broadcasted_iota(jnp.int32, (c, c), 0)
    col = lax.broadcasted_iota(jnp.int32, (c, c), 1)
    strict = row > col
    incl = row >= col
    tri = incl.astype(BF16)
    items = [(hh, pl.ds(ci * c, c)) for hh in range(nheads) for ci in range(tstep // c)]
    ld = lambda ref: [ref[0, hh, rows, :] for hh, rows in items]
    r, lw, k, v, kk, b = ld(r_ref), ld(lw_ref), ld(k_ref), ld(v_ref), ld(kk_ref), ld(b_ref)

    lw_hi = [x.astype(BF16) for x in lw]
    lc_hi = [_bdot(tri, h) for h in lw_hi]
    lc_lo = [_bdot(tri, x - h.astype(F32)) for x, h in zip(lw, lw_hi)]
    lc = [p + q for p, q in zip(lc_hi, lc_lo)]
    e_inc = [jnp.exp(x) for x in lc]
    e_inv = [jnp.exp(-x) for x in lc]
    e_end = [jnp.exp(x[c - 1:c, :] - x) for x in lc]
    at = [-q * jnp.exp(x - w) for q, x, w in zip(kk, lc, lw)]
    rt = [p * e for p, e in zip(r, e_inc)]
    m1 = [_bdot(jnp.concatenate([p, q], axis=0), jnp.concatenate([bb * e, kx * e], axis=0), _NT)
          for p, q, bb, kx, e in zip(at, rt, b, k, e_inv)]
    a_ab = [jnp.where(strict, m[:c, :c], 0.0) for m in m1]
    a_ak = [jnp.where(strict, m[:c, c:], 0.0) for m in m1]
    r_b = [jnp.where(incl, m[c:, :c], 0.0) for m in m1]
    r_k = [jnp.where(incl, m[c:, c:], 0.0) for m in m1]
    av = [_bdot(p, q) for p, q in zip(a_ak, v)]
    rkv = [_bdot(p, q) for p, q in zip(r_k, v)]
    kev = [_bdot(kx * e, q, _TN) for kx, e, q in zip(k, e_end, v)]
    xs = _unit_lower_inverses(a_ab, row, col)
    z = [_bdot(x, jnp.concatenate([p, q], axis=1)) for x, p, q in zip(xs, at, av)]
    rw = [_bdot(p, q) for p, q in zip(r_b, z)]
    tz = [_bdot(bb * e, q, _TN) for bb, e, q in zip(b, e_end, z)]
    eye_k = lax.broadcasted_iota(jnp.int32, (kd, kd), 0) == lax.broadcasted_iota(jnp.int32, (kd, kd), 1)
    for i, (hh, rows) in enumerate(items):
        rp_o[0, hh, rows, :] = rt[i] + rw[i][:, :kd]
        y0_o[0, hh, rows, :] = rw[i][:, kd:] + rkv[i]
        tm_o[0, hh, rows, :] = jnp.where(eye_k, jnp.broadcast_to(e_inc[i][c - 1:c, :], (kd, kd)), 0.0) + tz[i][:, :kd]
        sadd_o[0, hh, rows, :] = tz[i][:, kd:] + kev[i]
        bonus_o[0, hh, rows, :] = jnp.sum(r[i] * k[i] * rk_ref[hh], axis=-1, keepdims=True) * v[i]


def _rwchunk(r, lw, k, v, kk, b, r_k):
    b_, h_, t_, d_ = r.shape
    assert RW_CHUNK == d_
    hp = RW_MAP_HEADS
    tstep = min(RW_MAP_TSTEP, t_)
    seq = pl.BlockSpec((1, hp, tstep, d_), lambda bb, h, i: (bb, h, i, 0))
    par = pl.BlockSpec((hp, 1, d_), lambda bb, h, i: (h, 0, 0))
    return pl.pallas_call(
        _rwchunk_kernel,
        out_shape=[jax.ShapeDtypeStruct(r.shape, F32)] * 5,
        grid=(b_, h_ // hp, t_ // tstep),
        in_specs=[seq] * 6 + [par],
        out_specs=[seq] * 5,
        compiler_params=_params(("parallel", "parallel", "parallel")),
        name="rwkv_chunk",
    )(r, lw, k, v, kk, b, r_k)


def _rwstate_kernel(rp_ref, y0_ref, tm_ref, sadd_ref, bonus_ref, g_ref, gnw_ref, gnb_ref, o_ref, s_ref):
    @pl.when(pl.program_id(1) == 0)
    def _():
        s_ref[...] = jnp.zeros_like(s_ref)

    c = RW_CHUNK
    nheads, tstep, _ = rp_ref.shape[1:]
    heads = range(nheads)
    for ci in range(tstep // c):
        rows = pl.ds(ci * c, c)
        st = [s_ref[hh] for hh in heads]
        new = [_bdot(tm_ref[0, hh, rows, :], st[hh]) for hh in heads]
        ys = [_bdot(rp_ref[0, hh, rows, :], st[hh]) for hh in heads]
        outs = []
        for hh in heads:
            s_ref[hh] = new[hh] + sadd_ref[0, hh, rows, :]
            y = ys[hh] + y0_ref[0, hh, rows, :]
            mean = jnp.mean(y, axis=-1, keepdims=True)
            var = jnp.mean(jnp.square(y - mean), axis=-1, keepdims=True)
            yn = (y - mean) * lax.rsqrt(var + RW_GN_EPS) * gnw_ref[hh] + gnb_ref[hh]
            outs.append((yn + bonus_ref[0, hh, rows, :]) * g_ref[0, hh, rows, :])
        o_ref[0, rows, :] = jnp.concatenate(outs, axis=-1)


def _rwstate(rp, y0, tm, sadd, bonus, g, gn_w, gn_b):
    b_, h_, t_, d_ = rp.shape
    tstep = min(RW_TSTEP, t_)
    seq = pl.BlockSpec((1, h_, tstep, d_), lambda bb, i: (bb, 0, i, 0))
    par = pl.BlockSpec((h_, 1, d_), lambda bb, i: (0, 0, 0))
    return pl.pallas_call(
        _rwstate_kernel,
        out_shape=jax.ShapeDtypeStruct((b_, t_, h_ * d_), F32),
        grid=(b_, t_ // tstep),
        in_specs=[seq] * 6 + [par] * 2,
        out_specs=pl.BlockSpec((1, tstep, h_ * d_), lambda bb, i: (bb, i, 0)),
        scratch_shapes=[pltpu.VMEM((h_, d_, d_), F32)],
        compiler_params=_params(("parallel", "arbitrary")),
        name="rwkv_state",
    )(rp, y0, tm, sadd, bonus, g, gn_w, gn_b)


def _attn_kernel(q_ref, k_ref, v_ref, lq1_ref, lk1_ref, lq2_ref, lk2_ref, sw_ref, o_ref,
                 qs_ref, m_ref, acc_ref, *, lam_init):
    qi = pl.program_id(2)
    t = q_ref.shape[1]
    q = q_ref[0] * (DA_HD ** -0.5)
    lane = lax.broadcasted_iota(jnp.int32, q.shape, 1)
    qs_ref[0] = jnp.where(lane < DA_HD, q, 0.0).astype(BF16)
    qs_ref[1] = jnp.where(lane >= DA_HD, q, 0.0).astype(BF16)
    m_ref[...] = jnp.full_like(m_ref, NEG_BIG)
    acc_ref[...] = jnp.zeros_like(acc_ref)
    hw = q.shape[1]
    ones = jnp.ones((t, hw), BF16)
    subheads = range(2)

    def key_tile(j, diagonal):
        rows = pl.ds(pl.multiple_of(j * t, t), t)
        kb = k_ref[0, rows, :].astype(BF16)
        vb = jnp.concatenate([v_ref[0, rows, :].astype(BF16), ones], axis=1)
        s = [_dot_nt(qs_ref[c], kb) for c in subheads]
        if diagonal:
            causal = (lax.broadcasted_iota(jnp.int32, (t, t), 1) <= lax.broadcasted_iota(jnp.int32, (t, t), 0))
            s = [jnp.where(causal, x, NEG_BIG) for x in s]
        m_old = [m_ref[c] for c in subheads]
        m_new = [jnp.maximum(m, jnp.max(x, axis=-1, keepdims=True)) for m, x in zip(m_old, s)]
        p = [jnp.exp(x - jnp.concatenate([m] * (t // hw), axis=1)).astype(BF16) for x, m in zip(s, m_new)]
        pv = [_dot(x, vb) for x in p]
        for c in subheads:
            alpha = jnp.exp(m_old[c] - m_new[c])
            acc_ref[c] = jnp.concatenate([alpha, alpha], axis=1) * acc_ref[c] + pv[c]
            m_ref[c] = m_new[c]

    def below_diagonal(j, carry):
        key_tile(j, False)
        return carry

    lax.fori_loop(0, qi, below_diagonal, 0)
    key_tile(qi, True)

    lam = (jnp.exp(jnp.sum(lq1_ref[...] * lk1_ref[...], axis=-1, keepdims=True))
           - jnp.exp(jnp.sum(lq2_ref[...] * lk2_ref[...], axis=-1, keepdims=True)) + lam_init)
    o = acc_ref[0, :, :hw] / acc_ref[0, :, hw:] - lam * (acc_ref[1, :, :hw] / acc_ref[1, :, hw:])
    o = o * lax.rsqrt(jnp.mean(o * o, axis=-1, keepdims=True) + DA_SUBLN_EPS) * sw_ref[...]
    o_ref[0] = o * (1.0 - lam_init)


def _attn(z, lq1, lk1, lq2, lk2, subln_w, lam_init):
    b_, t_, _ = z.shape
    tq = min(ATT_TILE, t_)
    hw = 2 * DA_HD
    q0, k0, v0 = RW_COLS // hw, (RW_COLS + DA_W) // hw, (RW_COLS + 2 * DA_W) // hw
    small = lambda n: pl.BlockSpec((1, n), lambda b, h, qi: (0, 0))
    return pl.pallas_call(
        functools.partial(_attn_kernel, lam_init=lam_init),
        out_shape=jax.ShapeDtypeStruct((b_, t_, DA_W), F32),
        grid=(b_, DA_HEADS, t_ // tq),
        in_specs=[
            pl.BlockSpec((1, tq, hw), lambda b, h, qi: (b, qi, q0 + h)),
            pl.BlockSpec((1, t_, hw), lambda b, h, qi: (b, 0, k0 + h)),
            pl.BlockSpec((1, t_, hw), lambda b, h, qi: (b, 0, v0 + h)),
            small(DA_HD), small(DA_HD), small(DA_HD), small(DA_HD), small(hw),
        ],
        out_specs=pl.BlockSpec((1, tq, hw), lambda b, h, qi: (b, qi, h)),
        scratch_shapes=[pltpu.VMEM((2, tq, hw), BF16), pltpu.VMEM((2, tq, hw), F32),
                        pltpu.VMEM((2, tq, 2 * hw), F32)],
        compiler_params=_params(("parallel", "parallel", "arbitrary")),
        name="diff_attn",
    )(z, z, z, lq1, lk1, lq2, lk2, subln_w)


def _gelu_tanh(x):
    return 0.5 * x * (1.0 + jnp.tanh(math.sqrt(2.0 / math.pi) * (x + 0.044715 * (x * x * x))))


def _s5_kernel(u_ref, bmat_ref, pw_ref, step_ref, cmat_ref, d_ref, wglu_ref, bglu_ref, o_ref,
               xs_ref, carry_ref):
    @pl.when(pl.program_id(1) == 0)
    def _():
        carry_ref[...] = jnp.zeros_like(carry_ref)

    u = u_ref[0]
    tt = u.shape[0]
    n = SSM_N
    xs_ref[...] = _dot(u.astype(BF16), bmat_ref[...])
    rowi = lax.broadcasted_iota(jnp.int32, (SUBLANES, n), 0)
    pr, pi = pw_ref[0], pw_ref[1]

    def block(i, carry):
        cr, ci = carry
        rows = pl.ds(pl.multiple_of(i * SUBLANES, SUBLANES), SUBLANES)
        xr = xs_ref[rows, :n]
        xi = xs_ref[rows, n:]
        for lvl, d in enumerate((1, 2, 4)):
            ar, ai = step_ref[lvl:lvl + 1, :n], step_ref[lvl:lvl + 1, n:]
            sr = jnp.where(rowi >= d, pltpu.roll(xr, d, 0), 0.0)
            si = jnp.where(rowi >= d, pltpu.roll(xi, d, 0), 0.0)
            xr, xi = xr + ar * sr - ai * si, xi + ar * si + ai * sr
        xr, xi = xr + pr * cr - pi * ci, xi + pr * ci + pi * cr
        xs_ref[rows, :n] = xr
        xs_ref[rows, n:] = xi
        return xr[SUBLANES - 1:, :], xi[SUBLANES - 1:, :]

    cr, ci = lax.fori_loop(0, tt // SUBLANES, block, (carry_ref[0:1, :], carry_ref[1:2, :]))
    carry_ref[0:1, :] = cr
    carry_ref[1:2, :] = ci
    y = _dot(xs_ref[...].astype(BF16), cmat_ref[...]) + d_ref[...] * u
    y = _gelu_tanh(y)
    o_ref[0] = y * jax.nn.sigmoid(_dot(y.astype(BF16), wglu_ref[...]) + bglu_ref[...])


def _s5(z, bmat, pw, step, cmat, d_skip, w_glu, b_glu):
    b_, t_, _ = z.shape
    tt = min(SSM_TSTEP, t_)
    n = SSM_N
    full = lambda shape: pl.BlockSpec(shape, lambda b, i: (0,) * len(shape))
    return pl.pallas_call(
        _s5_kernel,
        out_shape=jax.ShapeDtypeStruct((b_, t_, SSM_W), F32),
        grid=(b_, t_ // tt),
        in_specs=[
            pl.BlockSpec((1, tt, SSM_W), lambda b, i: (b, i, SSM_COL0 // SSM_W)),
            full((SSM_W, 2 * n)), full((2, SUBLANES, n)), full((3, 2 * n)), full((2 * n, SSM_W)),
            full((1, SSM_W)), full((SSM_W, SSM_W)), full((1, SSM_W)),
        ],
        out_specs=pl.BlockSpec((1, tt, SSM_W), lambda b, i: (b, i, 0)),
        scratch_shapes=[pltpu.VMEM((tt, 2 * n), F32), pltpu.VMEM((2, n), F32)],
        compiler_params=_params(("parallel", "arbitrary")),
        name="s5",
    )(z, bmat, pw, step, cmat, d_skip, w_glu, b_glu)


def _s5_discretise(a_re, a_im, log_dt, b_re, b_im, c_re, c_im):
    g_, n_, c_ = SSM_GROUPS, SSM_STATE, SSM_GROUP
    dt = jnp.exp(log_dt)[:, None]
    mag = jnp.exp(dt * a_re)
    abar_r, abar_i = mag * jnp.cos(dt * a_im), mag * jnp.sin(dt * a_im)
    den = a_re * a_re + a_im * a_im
    nr, ni = abar_r - 1.0, abar_i
    coef_r, coef_i = (nr * a_re + ni * a_im) / den, (ni * a_re - nr * a_im) / den
    bbar_r = coef_r[..., None] * b_re - coef_i[..., None] * b_im
    bbar_i = coef_r[..., None] * b_im + coef_i[..., None] * b_re
    eye = jnp.eye(g_, dtype=F32)
    bd_in = lambda m: jnp.einsum('gnc,gh->gchn', m, eye).reshape(g_ * c_, g_ * n_)
    bmat = jnp.concatenate([bd_in(bbar_r), bd_in(bbar_i)], axis=1).astype(BF16)
    bd_out = lambda m: jnp.einsum('gcn,gh->gnhc', m, eye).reshape(g_ * n_, g_ * c_)
    cmat = jnp.concatenate([bd_out(c_re), -bd_out(c_im)], axis=0).astype(BF16)
    ar, ai = abar_r.reshape(1, -1), abar_i.reshape(1, -1)
    pows_r, pows_i = [ar], [ai]
    for _ in range(SUBLANES - 1):
        pr, pi = pows_r[-1], pows_i[-1]
        pows_r.append(pr * ar - pi * ai)
        pows_i.append(pr * ai + pi * ar)
    pw = jnp.stack([jnp.concatenate(pows_r, axis=0), jnp.concatenate(pows_i, axis=0)])
    step = jnp.concatenate([jnp.concatenate([pows_r[d - 1] for d in (1, 2, 4)], axis=0),
                            jnp.concatenate([pows_i[d - 1] for d in (1, 2, 4)], axis=0)], axis=1)
    return bmat, pw, step, cmat


def _outproj_kernel(x_ref, yrw_ref, yda_ref, yss_ref, w1_ref, w2_ref, w3_ref, g_ref, o_ref):
    y = (_dot(yrw_ref[...].astype(BF16), w1_ref[...]) + _dot(yda_ref[...].astype(BF16), w2_ref[...])
         + _dot(yss_ref[...].astype(BF16), w3_ref[...]))
    o_ref[...] = x_ref[...] + _rms(y, g_ref[...], NORM_EPS)


def _outproj(x, y_rw, y_da, y_ss, w, g):
    m, d = x.shape
    tm = min(ROW_TILE, m)
    rowblk = lambda n: pl.BlockSpec((tm, n), lambda i: (i, 0))
    assert RW_W == DA_W and (RW_W + DA_W) % SSM_W == 0
    band = lambda rows, idx: pl.BlockSpec((rows, d), lambda i: (idx, 0))
    return pl.pallas_call(
        _outproj_kernel,
        out_shape=jax.ShapeDtypeStruct((m, d), F32),
        grid=(m // tm,),
        in_specs=[rowblk(d), rowblk(RW_W), rowblk(DA_W), rowblk(SSM_W),
                  band(RW_W, 0), band(DA_W, 1), band(SSM_W, (RW_W + DA_W) // SSM_W),
                  pl.BlockSpec((1, d), lambda i: (0, 0))],
        out_specs=rowblk(d),
        compiler_params=_params(("parallel",)),
        name="outproj",
    )(x, y_rw, y_da, y_ss, w, w, w, g)


def kernel(x, ffn1_pre_g, ffn1_w_gu, ffn1_w_down, ffn1_post_g, mix_pre_g, w_in, rw_mu, rw_w0, rw_w2, rw_a0, rw_a2, rw_g2, rw_k_k, rw_k_a, rw_r_k, rw_gn_w, rw_gn_b, da_lq1, da_lk1, da_lq2, da_lk2, da_subln_w, ssm_a_re, ssm_a_im, ssm_log_dt, ssm_b_re, ssm_b_im, ssm_c_re, ssm_c_im, ssm_d, ssm_w_glu, ssm_b_glu, w_out, mix_post_g, ffn2_pre_g, ffn2_w_gu, ffn2_w_down, ffn2_post_g):
    b_, t_, d_ = x.shape
    m = b_ * t_
    row = lambda a: a.reshape(1, -1)
    xf = x.reshape(m, d_)
    for l in range(DEPTH):
        xf = _ffn(xf, row(ffn1_pre_g[l]), ffn1_w_gu, ffn1_w_down, row(ffn1_post_g[l]), l)

        z = _inproj(xf, row(mix_pre_g[l]), w_in, l).reshape(b_, t_, IN_COLS_PAD)

        feats = _rwprep(z, row(rw_mu[l]), row(rw_w0[l]), rw_w2[l], row(rw_a0[l]), rw_a2[l], rw_g2[l],
                        row(rw_k_k[l]), row(rw_k_a[l]))
        per_head = lambda a: a.reshape(RW_HEADS, 1, RW_HD)
        r, lw, k, v, kk, b, g = feats
        maps = _rwchunk(r, lw, k, v, kk, b, per_head(rw_r_k[l]))
        y_rw = _rwstate(*maps, g, per_head(rw_gn_w[l]), per_head(rw_gn_b[l]))

        lam_init = 0.8 - 0.6 * math.exp(-0.3 * l)
        y_da = _attn(z, row(da_lq1[l]), row(da_lk1[l]), row(da_lq2[l]), row(da_lk2[l]),
                     row(da_subln_w[l]), lam_init)

        bmat, pw, step, cmat = _s5_discretise(ssm_a_re[l], ssm_a_im[l], ssm_log_dt[l], ssm_b_re[l],
                                              ssm_b_im[l], ssm_c_re[l], ssm_c_im[l])
        y_ss = _s5(z, bmat, pw, step, cmat, row(ssm_d[l]), ssm_w_glu[l].astype(BF16), row(ssm_b_glu[l]))

        xf = _outproj(xf, y_rw.reshape(m, RW_W), y_da.reshape(m, DA_W), y_ss.reshape(m, SSM_W),
                      w_out[l].astype(BF16), row(mix_post_g[l]))

        xf = _ffn(xf, row(ffn2_pre_g[l]), ffn2_w_gu, ffn2_w_down, row(ffn2_post_g[l]), l)
    return xf.reshape(b_, t_, d_)
```

```python
import functools
import math

import jax
import jax.numpy as jnp
from jax import lax
from jax.experimental import pallas as pl
from jax.experimental.pallas import tpu as pltpu

F32 = jnp.float32
BF16 = jnp.bfloat16
HIGHEST = lax.Precision.HIGHEST

D_MODEL = 2048
DEPTH = 2
RW_HEADS = 12
RW_HD = 64
RW_W = RW_HEADS * RW_HD
RW_DECAY_LORA = 64
RW_AAA_LORA = 64
RW_GATE_LORA = 128
RW_GN_EPS = 64e-5
DA_HEADS = 6
DA_HD = 64
DA_W = DA_HEADS * 2 * DA_HD
DA_SUBLN_EPS = 1e-5
SSM_W = D_MODEL - RW_W - DA_W
SSM_GROUP = 16
SSM_GROUPS = SSM_W // SSM_GROUP
SSM_STATE = 64
SSM_N = SSM_GROUPS * SSM_STATE
RW_COLS = 3 * RW_W + RW_DECAY_LORA + RW_AAA_LORA + RW_GATE_LORA
DA_COLS = 3 * DA_W
D_FF = 5504
NORM_EPS = 1e-6

LANES = 128
SUBLANES = 8
VMEM_LIMIT = 56 * 1024 * 1024
FF_TILE = 256
FFN_ROW_TILE = 1024
IN_ROW_TILE = 1024
IN_TILE = 512
SSM_COL0 = -(-(RW_COLS + DA_COLS) // IN_TILE) * IN_TILE
IN_COLS_PAD = SSM_COL0 + SSM_W
ROW_TILE = 512
RW_CHUNK = 64
RW_TSTEP = 256
RW_MAP_HEADS = 4
RW_MAP_TSTEP = 512
ATT_TILE = 512
SSM_TSTEP = 256
PREP_TSTEP = 256
NEG_BIG = -1e30


def _dot(a, b, precision=None):
    return jnp.dot(a, b, preferred_element_type=F32, precision=precision)


def _dot_nt(a, b, precision=None):
    return lax.dot_general(a, b, (((1,), (1,)), ((), ())),
                           preferred_element_type=F32, precision=precision)


def _rms(x, g, eps):
    return x * lax.rsqrt(jnp.mean(x * x, axis=-1, keepdims=True) + eps) * g


def _params(sem):
    return pltpu.CompilerParams(dimension_semantics=sem, vmem_limit_bytes=VMEM_LIMIT)


def _ff_offset(j, base=0):
    return (base // LANES + jnp.minimum(j * (FF_TILE // LANES), (D_FF - FF_TILE) // LANES)) * LANES


def _ffn_kernel(x_ref, pre_ref, wg_ref, wu_ref, wd_ref, post_ref, o_ref, xn_ref):
    j = pl.program_id(1)

    @pl.when(j == 0)
    def _():
        xn_ref[...] = _rms(x_ref[...], pre_ref[...], NORM_EPS).astype(BF16)
        o_ref[...] = jnp.zeros_like(o_ref)

    xn = xn_ref[...]
    gate = _dot(xn, wg_ref[0].astype(BF16))
    up = _dot(xn, wu_ref[0].astype(BF16))
    h = gate * jax.nn.sigmoid(gate) * up
    unit = _ff_offset(j) + lax.broadcasted_iota(jnp.int32, h.shape, 1)
    h = jnp.where(unit >= j * FF_TILE, h, 0.0).astype(BF16)
    o_ref[...] += _dot(h, wd_ref[0].astype(BF16))

    @pl.when(j == pl.num_programs(1) - 1)
    def _():
        o_ref[...] = x_ref[...] + 0.5 * _rms(o_ref[...], post_ref[...], NORM_EPS)


def _ffn(x, pre_g, w_gu, w_down, post_g, layer):
    m, d = x.shape
    tm, tf = min(FFN_ROW_TILE, m), FF_TILE
    once = pl.Buffered(1)
    return pl.pallas_call(
        _ffn_kernel,
        out_shape=jax.ShapeDtypeStruct((m, d), F32),
        grid=(m // tm, pl.cdiv(D_FF, tf)),
        in_specs=[
            pl.BlockSpec((tm, d), lambda i, j: (i, 0)),
            pl.BlockSpec((1, d), lambda i, j: (0, 0)),
            pl.BlockSpec((pl.Element(1), pl.Element(d), pl.Element(tf)), lambda i, j: (layer, 0, _ff_offset(j))),
            pl.BlockSpec((pl.Element(1), pl.Element(d), pl.Element(tf)), lambda i, j: (layer, 0, _ff_offset(j, D_FF))),
            pl.BlockSpec((pl.Element(1), pl.Element(tf), pl.Element(d)), lambda i, j: (layer, _ff_offset(j), 0)),
            pl.BlockSpec((1, d), lambda i, j: (0, 0)),
        ],
        out_specs=pl.BlockSpec((tm, d), lambda i, j: (i, 0), pipeline_mode=once),
        scratch_shapes=[pltpu.VMEM((tm, d), BF16)],
        compiler_params=_params(("parallel", "arbitrary")),
        name="ffn",
    )(x, pre_g, w_gu, w_gu, w_down, post_g)


def _inproj_kernel(x_ref, g_ref, w_ref, o_ref, xn_ref):
    @pl.when(pl.program_id(1) == 0)
    def _():
        xn_ref[...] = _rms(x_ref[...], g_ref[...], NORM_EPS).astype(BF16)

    o_ref[...] = _dot(xn_ref[...], w_ref[0].astype(BF16))


def _inproj(x, g, w_in, layer):
    m, d = x.shape
    tm, tn = min(IN_ROW_TILE, m), IN_TILE
    gap = SSM_COL0 - (RW_COLS + DA_COLS)

    def w_col(j):
        return (j * (tn // LANES) - jnp.where(j * tn >= SSM_COL0, gap // LANES, 0)) * LANES

    return pl.pallas_call(
        _inproj_kernel,
        out_shape=jax.ShapeDtypeStruct((m, IN_COLS_PAD), F32),
        grid=(m // tm, IN_COLS_PAD // tn),
        in_specs=[
            pl.BlockSpec((tm, d), lambda i, j: (i, 0)),
            pl.BlockSpec((1, d), lambda i, j: (0, 0)),
            pl.BlockSpec((pl.Element(1), pl.Element(d), pl.Element(tn)), lambda i, j: (layer, 0, w_col(j))),
        ],
        out_specs=pl.BlockSpec((tm, tn), lambda i, j: (i, j)),
        scratch_shapes=[pltpu.VMEM((tm, d), BF16)],
        compiler_params=_params(("parallel", "arbitrary")),
        name="inproj",
    )(x, g, w_in)


def _softplus(x):
    return jnp.maximum(x, 0.0) + jnp.log(1.0 + jnp.exp(-jnp.abs(x)))


def _rwprep_kernel(z_ref, zp_ref, mu_ref, w0_ref, w2_ref, a0_ref, a2_ref, g2_ref, kk_ref, ka_ref,
                   r_o, lw_o, k_o, v_o, kk_o, b_o, g_o):
    i = pl.program_id(1)
    z = z_ref[0]
    tt = z.shape[0]
    last_prev = jnp.where(i == 0, 0.0, zp_ref[0, SUBLANES - 1:SUBLANES, :])
    row = lax.broadcasted_iota(jnp.int32, z.shape, 0)
    z_prev = jnp.where(row == 0, last_prev, pltpu.roll(z, 1, 0))
    z = z + (z_prev - z) * mu_ref[...]
    o1, o2, o3 = RW_W, 2 * RW_W, 3 * RW_W
    o4, o5 = o3 + RW_DECAY_LORA, o3 + RW_DECAY_LORA + RW_AAA_LORA
    r, k, v = z[:, :o1], z[:, o1:o2], z[:, o2:o3]
    zw, za, zg = z[:, o3:o4], z[:, o4:o5], z[:, o5:]
    w = -_softplus(-(w0_ref[...] + _dot(jnp.tanh(zw), w2_ref[...], HIGHEST))) - 0.5
    lw = -jnp.exp(w)
    a = jax.nn.sigmoid(a0_ref[...] + _dot(za, a2_ref[...], HIGHEST))
    g = _dot(jax.nn.sigmoid(zg), g2_ref[...], HIGHEST)
    kk = k * kk_ref[...]
    k = k * (1.0 + (a - 1.0) * ka_ref[...])
    for h in range(RW_HEADS):
        sl = slice(h * RW_HD, (h + 1) * RW_HD)
        kk_h = kk[:, sl]
        nrm = jnp.sqrt(jnp.sum(kk_h * kk_h, axis=-1, keepdims=True))
        kk_h = kk_h / jnp.maximum(nrm, 1e-12)
        r_o[0, h] = r[:, sl]
        lw_o[0, h] = lw[:, sl]
        k_o[0, h] = k[:, sl]
        v_o[0, h] = v[:, sl]
        kk_o[0, h] = kk_h
        b_o[0, h] = kk_h * a[:, sl]
        g_o[0, h] = g[:, sl]


def _rwprep(z, mu, w0, w2, a0, a2, g2, k_k, k_a):
    b_, t_, _ = z.shape
    tt = min(PREP_TSTEP, t_)
    nprev = tt // SUBLANES
    full = lambda shape: pl.BlockSpec(shape, lambda b, i: (0,) * len(shape))
    head_out = jax.ShapeDtypeStruct((b_, RW_HEADS, t_, RW_HD), F32)
    head_spec = pl.BlockSpec((1, RW_HEADS, tt, RW_HD), lambda b, i: (b, 0, i, 0))
    return pl.pallas_call(
        _rwprep_kernel,
        out_shape=[head_out] * 7,
        grid=(b_, t_ // tt),
        in_specs=[
            pl.BlockSpec((1, tt, RW_COLS), lambda b, i: (b, i, 0)),
            pl.BlockSpec((1, SUBLANES, RW_COLS), lambda b, i: (b, jnp.maximum(i * nprev - 1, 0), 0)),
            full((1, RW_COLS)), full((1, RW_W)), full((RW_DECAY_LORA, RW_W)), full((1, RW_W)),
            full((RW_AAA_LORA, RW_W)), full((RW_GATE_LORA, RW_W)), full((1, RW_W)), full((1, RW_W)),
        ],
        out_specs=[head_spec] * 7,
        compiler_params=_params(("parallel", "arbitrary")),
        name="rwkv_features",
    )(z, z, mu, w0, w2, a0, a2, g2, k_k, k_a)


_NN = ((1,), (0,))
_NT = ((1,), (1,))
_TN = ((0,), (0,))


def _bdot(a, b, dims=_NN):
    return lax.dot_general(a.astype(BF16), b.astype(BF16), (dims, ((), ())), preferred_element_type=F32)


def _unit_lower_inverses(ns, row, col):
    c = ns[0].shape[0]
    eye = (row == col).astype(F32)
    base = SUBLANES
    diag = (row // base) == (col // base)
    a0 = [jnp.where(diag, n, 0.0) for n in ns]
    a2 = [_bdot(a, a) for a in a0]
    a4 = [_bdot(a, a) for a in a2]
    xs = [_bdot(eye + p, eye + q) for p, q in zip(a0, a2)]
    xs = [_bdot(p, eye + q) for p, q in zip(xs, a4)]
    m = base
    while m < c:
        off = ((row // (2 * m)) == (col // (2 * m))) & ((row // m) != (col // m))
        xe = [_bdot(p, jnp.where(off, n, 0.0)) for p, n in zip(xs, ns)]
        xs = [p + _bdot(q, p) for p, q in zip(xs, xe)]
        m *= 2
    return xs


def _rwchunk_kernel(r_ref, lw_ref, k_ref, v_ref, kk_ref, b_ref, rk_ref, rp_o, y0_o, tm_o, sadd_o, bonus_o):
    c = RW_CHUNK
    nheads, tstep, kd = r_ref.shape[1:]
    row = lax.broadcasted_iota(jnp.int32, (c, c), 0)
    col = lax.broadcasted_iota(jnp.int32, (c, c), 1)
    strict = row > col
    incl = row >= col
    tri = incl.astype(BF16)
    items = [(hh, pl.ds(ci * c, c)) for hh in range(nheads) for ci in range(tstep // c)]
    ld = lambda ref: [ref[0, hh, rows, :] for hh, rows in items]
    r, lw, k, v, kk, b = ld(r_ref), ld(lw_ref), ld(k_ref), ld(v_ref), ld(kk_ref), ld(b_ref)

    lw_hi = [x.astype(BF16) for x in lw]
    lc_hi = [_bdot(tri, h) for h in lw_hi]
    lc_lo = [_bdot(tri, x - h.astype(F32)) for x, h in zip(lw, lw_hi)]
    lc = [p + q for p, q in zip(lc_hi, lc_lo)]
    e_inc = [jnp.exp(x) for x in lc]
    e_inv = [jnp.exp(-x) for x in lc]
    e_end = [jnp.exp(x[c - 1:c, :] - x) for x in lc]
    at = [-q * jnp.exp(x - w) for q, x, w in zip(kk, lc, lw)]
    rt = [p * e for p, e in zip(r, e_inc)]
    m1 = [_bdot(jnp.concatenate([p, q], axis=0), jnp.concatenate([bb * e, kx * e], axis=0), _NT)
          for p, q, bb, kx, e in zip(at, rt, b, k, e_inv)]
    a_ab = [jnp.where(strict, m[:c, :c], 0.0) for m in m1]
    a_ak = [jnp.where(strict, m[:c, c:], 0.0) for m in m1]
    r_b = [jnp.where(incl, m[c:, :c], 0.0) for m in m1]
    r_k = [jnp.where(incl, m[c:, c:], 0.0) for m in m1]
    av = [_bdot(p, q) for p, q in zip(a_ak, v)]
    rkv = [_bdot(p, q) for p, q in zip(r_k, v)]
    kev = [_bdot(kx * e, q, _TN) for kx, e, q in zip(k, e_end, v)]
    xs = _unit_lower_inverses(a_ab, row, col)
    z = [_bdot(x, jnp.concatenate([p, q], axis=1)) for x, p, q in zip(xs, at, av)]
    rw = [_bdot(p, q) for p, q in zip(r_b, z)]
    tz = [_bdot(bb * e, q, _TN) for bb, e, q in zip(b, e_end, z)]
    eye_k = lax.broadcasted_iota(jnp.int32, (kd, kd), 0) == lax.broadcasted_iota(jnp.int32, (kd, kd), 1)
    for i, (hh, rows) in enumerate(items):
        rp_o[0, hh, rows, :] = rt[i] + rw[i][:, :kd]
        y0_o[0, hh, rows, :] = rw[i][:, kd:] + rkv[i]
        tm_o[0, hh, rows, :] = jnp.where(eye_k, jnp.broadcast_to(e_inc[i][c - 1:c, :], (kd, kd)), 0.0) + tz[i][:, :kd]
        sadd_o[0, hh, rows, :] = tz[i][:, kd:] + kev[i]
        bonus_o[0, hh, rows, :] = jnp.sum(r[i] * k[i] * rk_ref[hh], axis=-1, keepdims=True) * v[i]


def _rwchunk(r, lw, k, v, kk, b, r_k):
    b_, h_, t_, d_ = r.shape
    assert RW_CHUNK == d_
    hp = RW_MAP_HEADS
    tstep = min(RW_MAP_TSTEP, t_)
    seq = pl.BlockSpec((1, hp, tstep, d_), lambda bb, h, i: (bb, h, i, 0))
    par = pl.BlockSpec((hp, 1, d_), lambda bb, h, i: (h, 0, 0))
    return pl.pallas_call(
        _rwchunk_kernel,
        out_shape=[jax.ShapeDtypeStruct(r.shape, F32)] * 5,
        grid=(b_, h_ // hp, t_ // tstep),
        in_specs=[seq] * 6 + [par],
        out_specs=[seq] * 5,
        compiler_params=_params(("parallel", "parallel", "parallel")),
        name="rwkv_chunk",
    )(r, lw, k, v, kk, b, r_k)


def _rwstate_kernel(rp_ref, y0_ref, tm_ref, sadd_ref, bonus_ref, g_ref, gnw_ref, gnb_ref, o_ref, s_ref):
    @pl.when(pl.program_id(1) == 0)
    def _():
        s_ref[...] = jnp.zeros_like(s_ref)

    c = RW_CHUNK
    nheads, tstep, _ = rp_ref.shape[1:]
    heads = range(nheads)
    for ci in range(tstep // c):
        rows = pl.ds(ci * c, c)
        st = [s_ref[hh] for hh in heads]
        new = [_bdot(tm_ref[0, hh, rows, :], st[hh]) for hh in heads]
        ys = [_bdot(rp_ref[0, hh, rows, :], st[hh]) for hh in heads]
        outs = []
        for hh in heads:
            s_ref[hh] = new[hh] + sadd_ref[0, hh, rows, :]
            y = ys[hh] + y0_ref[0, hh, rows, :]
            mean = jnp.mean(y, axis=-1, keepdims=True)
            var = jnp.mean(jnp.square(y - mean), axis=-1, keepdims=True)
            yn = (y - mean) * lax.rsqrt(var + RW_GN_EPS) * gnw_ref[hh] + gnb_ref[hh]
            outs.append((yn + bonus_ref[0, hh, rows, :]) * g_ref[0, hh, rows, :])
        o_ref[0, rows, :] = jnp.concatenate(outs, axis=-1)


def _rwstate(rp, y0, tm, sadd, bonus, g, gn_w, gn_b):
    b_, h_, t_, d_ = rp.shape
    tstep = min(RW_TSTEP, t_)
    seq = pl.BlockSpec((1, h_, tstep, d_), lambda bb, i: (bb, 0, i, 0))
    par = pl.BlockSpec((h_, 1, d_), lambda bb, i: (0, 0, 0))
    return pl.pallas_call(
        _rwstate_kernel,
        out_shape=jax.ShapeDtypeStruct((b_, t_, h_ * d_), F32),
        grid=(b_, t_ // tstep),
        in_specs=[seq] * 6 + [par] * 2,
        out_specs=pl.BlockSpec((1, tstep, h_ * d_), lambda bb, i: (bb, i, 0)),
        scratch_shapes=[pltpu.VMEM((h_, d_, d_), F32)],
        compiler_params=_params(("parallel", "arbitrary")),
        name="rwkv_state",
    )(rp, y0, tm, sadd, bonus, g, gn_w, gn_b)


def _attn_kernel(q_ref, k_ref, v_ref, lq1_ref, lk1_ref, lq2_ref, lk2_ref, sw_ref, o_ref,
                 qs_ref, m_ref, acc_ref, *, lam_init):
    qi = pl.program_id(2)
    t = q_ref.shape[1]
    q = q_ref[0] * (DA_HD ** -0.5)
    lane = lax.broadcasted_iota(jnp.int32, q.shape, 1)
    qs_ref[0] = jnp.where(lane < DA_HD, q, 0.0).astype(BF16)
    qs_ref[1] = jnp.where(lane >= DA_HD, q, 0.0).astype(BF16)
    m_ref[...] = jnp.full_like(m_ref, NEG_BIG)
    acc_ref[...] = jnp.zeros_like(acc_ref)
    hw = q.shape[1]
    ones = jnp.ones((t, hw), BF16)
    subheads = range(2)

    def key_tiles(js, diagonal):
        kbs, vbs = [], []
        for j in js:
            rows = pl.ds(pl.multiple_of(j * t, t), t)
            kbs.append(k_ref[0, rows, :].astype(BF16))
            vbs.append(jnp.concatenate([v_ref[0, rows, :].astype(BF16), ones], axis=1))
        s = [[_dot_nt(qs_ref[c], kb) for kb in kbs] for c in subheads]
        if diagonal:
            causal = (lax.broadcasted_iota(jnp.int32, (t, t), 1) <= lax.broadcasted_iota(jnp.int32, (t, t), 0))
            for c in subheads:
                s[c][-1] = jnp.where(causal, s[c][-1], NEG_BIG)
        m_old = [m_ref[c] for c in subheads]
        m_new = []
        for c in subheads:
            m = m_old[c]
            for x in s[c]:
                m = jnp.maximum(m, jnp.max(x, axis=-1, keepdims=True))
            m_new.append(m)
        m_wide = [jnp.concatenate([m] * (t // hw), axis=1) for m in m_new]
        p = [[jnp.exp(x - m_wide[c]).astype(BF16) for x in s[c]] for c in subheads]
        pv = [sum(_dot(x, vb) for x, vb in zip(p[c], vbs)) for c in subheads]
        for c in subheads:
            alpha = jnp.exp(m_old[c] - m_new[c])
            acc_ref[c] = jnp.concatenate([alpha, alpha], axis=1) * acc_ref[c] + pv[c]
            m_ref[c] = m_new[c]

    def pair_below_diagonal(jj, carry):
        key_tiles([2 * jj, 2 * jj + 1], False)
        return carry

    lax.fori_loop(0, qi // 2, pair_below_diagonal, 0)

    @pl.when(qi % 2 == 1)
    def _():
        key_tiles([qi - 1], False)

    key_tiles([qi], True)

    lam = (jnp.exp(jnp.sum(lq1_ref[...] * lk1_ref[...], axis=-1, keepdims=True))
           - jnp.exp(jnp.sum(lq2_ref[...] * lk2_ref[...], axis=-1, keepdims=True)) + lam_init)
    o = acc_ref[0, :, :hw] / acc_ref[0, :, hw:] - lam * (acc_ref[1, :, :hw] / acc_ref[1, :, hw:])
    o = o * lax.rsqrt(jnp.mean(o * o, axis=-1, keepdims=True) + DA_SUBLN_EPS) * sw_ref[...]
    o_ref[0] = o * (1.0 - lam_init)


def _attn(z, lq1, lk1, lq2, lk2, subln_w, lam_init):
    b_, t_, _ = z.shape
    tq = min(ATT_TILE, t_)
    hw = 2 * DA_HD
    q0, k0, v0 = RW_COLS // hw, (RW_COLS + DA_W) // hw, (RW_COLS + 2 * DA_W) // hw
    small = lambda n: pl.BlockSpec((1, n), lambda b, h, qi: (0, 0))
    return pl.pallas_call(
        functools.partial(_attn_kernel, lam_init=lam_init),
        out_shape=jax.ShapeDtypeStruct((b_, t_, DA_W), F32),
        grid=(b_, DA_HEADS, t_ // tq),
        in_specs=[
            pl.BlockSpec((1, tq, hw), lambda b, h, qi: (b, qi, q0 + h)),
            pl.BlockSpec((1, t_, hw), lambda b, h, qi: (b, 0, k0 + h)),
            pl.BlockSpec((1, t_, hw), lambda b, h, qi: (b, 0, v0 + h)),
            small(DA_HD), small(DA_HD), small(DA_HD), small(DA_HD), small(hw),
        ],
        out_specs=pl.BlockSpec((1, tq, hw), lambda b, h, qi: (b, qi, h)),
        scratch_shapes=[pltpu.VMEM((2, tq, hw), BF16), pltpu.VMEM((2, tq, hw), F32),
                        pltpu.VMEM((2, tq, 2 * hw), F32)],
        compiler_params=_params(("parallel", "parallel", "arbitrary")),
        name="diff_attn",
    )(z, z, z, lq1, lk1, lq2, lk2, subln_w)


def _gelu_tanh(x):
    return 0.5 * x * (1.0 + jnp.tanh(math.sqrt(2.0 / math.pi) * (x + 0.044715 * (x * x * x))))


def _s5_kernel(u_ref, bmat_ref, pw_ref, step_ref, cmat_ref, d_ref, wglu_ref, bglu_ref, o_ref,
               xs_ref, carry_ref):
    @pl.when(pl.program_id(1) == 0)
    def _():
        carry_ref[...] = jnp.zeros_like(carry_ref)

    u = u_ref[0]
    tt = u.shape[0]
    n = SSM_N
    xs_ref[...] = _dot(u.astype(BF16), bmat_ref[...])
    rowi = lax.broadcasted_iota(jnp.int32, (SUBLANES, n), 0)
    pr, pi = pw_ref[0], pw_ref[1]

    def block(i, carry):
        cr, ci = carry
        rows = pl.ds(pl.multiple_of(i * SUBLANES, SUBLANES), SUBLANES)
        xr = xs_ref[rows, :n]
        xi = xs_ref[rows, n:]
        for lvl, d in enumerate((1, 2, 4)):
            ar, ai = step_ref[lvl:lvl + 1, :n], step_ref[lvl:lvl + 1, n:]
            sr = jnp.where(rowi >= d, pltpu.roll(xr, d, 0), 0.0)
            si = jnp.where(rowi >= d, pltpu.roll(xi, d, 0), 0.0)
            xr, xi = xr + ar * sr - ai * si, xi + ar * si + ai * sr
        xr, xi = xr + pr * cr - pi * ci, xi + pr * ci + pi * cr
        xs_ref[rows, :n] = xr
        xs_ref[rows, n:] = xi
        return xr[SUBLANES - 1:, :], xi[SUBLANES - 1:, :]

    cr, ci = lax.fori_loop(0, tt // SUBLANES, block, (carry_ref[0:1, :], carry_ref[1:2, :]))
    carry_ref[0:1, :] = cr
    carry_ref[1:2, :] = ci
    y = _dot(xs_ref[...].astype(BF16), cmat_ref[...]) + d_ref[...] * u
    y = _gelu_tanh(y)
    o_ref[0] = y * jax.nn.sigmoid(_dot(y.astype(BF16), wglu_ref[...]) + bglu_ref[...])


def _s5(z, bmat, pw, step, cmat, d_skip, w_glu, b_glu):
    b_, t_, _ = z.shape
    tt = min(SSM_TSTEP, t_)
    n = SSM_N
    full = lambda shape: pl.BlockSpec(shape, lambda b, i: (0,) * len(shape))
    return pl.pallas_call(
        _s5_kernel,
        out_shape=jax.ShapeDtypeStruct((b_, t_, SSM_W), F32),
        grid=(b_, t_ // tt),
        in_specs=[
            pl.BlockSpec((1, tt, SSM_W), lambda b, i: (b, i, SSM_COL0 // SSM_W)),
            full((SSM_W, 2 * n)), full((2, SUBLANES, n)), full((3, 2 * n)), full((2 * n, SSM_W)),
            full((1, SSM_W)), full((SSM_W, SSM_W)), full((1, SSM_W)),
        ],
        out_specs=pl.BlockSpec((1, tt, SSM_W), lambda b, i: (b, i, 0)),
        scratch_shapes=[pltpu.VMEM((tt, 2 * n), F32), pltpu.VMEM((2, n), F32)],
        compiler_params=_params(("parallel", "arbitrary")),
        name="s5",
    )(z, bmat, pw, step, cmat, d_skip, w_glu, b_glu)


def _s5_discretise(a_re, a_im, log_dt, b_re, b_im, c_re, c_im):
    g_, n_, c_ = SSM_GROUPS, SSM_STATE, SSM_GROUP
    dt = jnp.exp(log_dt)[:, None]
    mag = jnp.exp(dt * a_re)
    abar_r, abar_i = mag * jnp.cos(dt * a_im), mag * jnp.sin(dt * a_im)
    den = a_re * a_re + a_im * a_im
    nr, ni = abar_r - 1.0, abar_i
    coef_r, coef_i = (nr * a_re + ni * a_im) / den, (ni * a_re - nr * a_im) / den
    bbar_r = coef_r[..., None] * b_re - coef_i[..., None] * b_im
    bbar_i = coef_r[..., None] * b_im + coef_i[..., None] * b_re
    eye = jnp.eye(g_, dtype=F32)
    bd_in = lambda m: jnp.einsum('gnc,gh->gchn', m, eye).reshape(g_ * c_, g_ * n_)
    bmat = jnp.concatenate([bd_in(bbar_r), bd_in(bbar_i)], axis=1).astype(BF16)
    bd_out = lambda m: jnp.einsum('gcn,gh->gnhc', m, eye).reshape(g_ * n_, g_ * c_)
    cmat = jnp.concatenate([bd_out(c_re), -bd_out(c_im)], axis=0).astype(BF16)
    ar, ai = abar_r.reshape(1, -1), abar_i.reshape(1, -1)
    pows_r, pows_i = [ar], [ai]
    for _ in range(SUBLANES - 1):
        pr, pi = pows_r[-1], pows_i[-1]
        pows_r.append(pr * ar - pi * ai)
        pows_i.append(pr * ai + pi * ar)
    pw = jnp.stack([jnp.concatenate(pows_r, axis=0), jnp.concatenate(pows_i, axis=0)])
    step = jnp.concatenate([jnp.concatenate([pows_r[d - 1] for d in (1, 2, 4)], axis=0),
                            jnp.concatenate([pows_i[d - 1] for d in (1, 2, 4)], axis=0)], axis=1)
    return bmat, pw, step, cmat


def _outproj_kernel(x_ref, yrw_ref, yda_ref, yss_ref, w1_ref, w2_ref, w3_ref, g_ref, o_ref):
    y = (_dot(yrw_ref[...].astype(BF16), w1_ref[...]) + _dot(yda_ref[...].astype(BF16), w2_ref[...])
         + _dot(yss_ref[...].astype(BF16), w3_ref[...]))
    o_ref[...] = x_ref[...] + _rms(y, g_ref[...], NORM_EPS)


def _outproj(x, y_rw, y_da, y_ss, w, g):
    m, d = x.shape
    tm = min(ROW_TILE, m)
    rowblk = lambda n: pl.BlockSpec((tm, n), lambda i: (i, 0))
    assert RW_W == DA_W and (RW_W + DA_W) % SSM_W == 0
    band = lambda rows, idx: pl.BlockSpec((rows, d), lambda i: (idx, 0))
    return pl.pallas_call(
        _outproj_kernel,
        out_shape=jax.ShapeDtypeStruct((m, d), F32),
        grid=(m // tm,),
        in_specs=[rowblk(d), rowblk(RW_W), rowblk(DA_W), rowblk(SSM_W),
                  band(RW_W, 0), band(DA_W, 1), band(SSM_W, (RW_W + DA_W) // SSM_W),
                  pl.BlockSpec((1, d), lambda i: (0, 0))],
        out_specs=rowblk(d),
        compiler_params=_params(("parallel",)),
        name="outproj",
    )(x, y_rw, y_da, y_ss, w, w, w, g)


def kernel(x, ffn1_pre_g, ffn1_w_gu, ffn1_w_down, ffn1_post_g, mix_pre_g, w_in, rw_mu, rw_w0, rw_w2, rw_a0, rw_a2, rw_g2, rw_k_k, rw_k_a, rw_r_k, rw_gn_w, rw_gn_b, da_lq1, da_lk1, da_lq2, da_lk2, da_subln_w, ssm_a_re, ssm_a_im, ssm_log_dt, ssm_b_re, ssm_b_im, ssm_c_re, ssm_c_im, ssm_d, ssm_w_glu, ssm_b_glu, w_out, mix_post_g, ffn2_pre_g, ffn2_w_gu, ffn2_w_down, ffn2_post_g):
    b_, t_, d_ = x.shape
    m = b_ * t_
    row = lambda a: a.reshape(1, -1)
    xf = x.reshape(m, d_)
    for l in range(DEPTH):
        xf = _ffn(xf, row(ffn1_pre_g[l]), ffn1_w_gu, ffn1_w_down, row(ffn1_post_g[l]), l)

        z = _inproj(xf, row(mix_pre_g[l]), w_in, l).reshape(b_, t_, IN_COLS_PAD)

        feats = _rwprep(z, row(rw_mu[l]), row(rw_w0[l]), rw_w2[l], row(rw_a0[l]), rw_a2[l], rw_g2[l],
                        row(rw_k_k[l]), row(rw_k_a[l]))
        per_head = lambda a: a.reshape(RW_HEADS, 1, RW_HD)
        r, lw, k, v, kk, b, g = feats
        maps = _rwchunk(r, lw, k, v, kk, b, per_head(rw_r_k[l]))
        y_rw = _rwstate(*maps, g, per_head(rw_gn_w[l]), per_head(rw_gn_b[l]))

        lam_init = 0.8 - 0.6 * math.exp(-0.3 * l)
        y_da = _attn(z, row(da_lq1[l]), row(da_lk1[l]), row(da_lq2[l]), row(da_lk2[l]),
                     row(da_subln_w[l]), lam_init)

        bmat, pw, step, cmat = _s5_discretise(ssm_a_re[l], ssm_a_im[l], ssm_log_dt[l], ssm_b_re[l],
                                              ssm_b_im[l], ssm_c_re[l], ssm_c_im[l])
        y_ss = _s5(z, bmat, pw, step, cmat, row(ssm_d[l]), ssm_w_glu[l].astype(BF16), row(ssm_b_glu[l]))

        xf = _outproj(xf, y_rw.reshape(m, RW_W), y_da.reshape(m, DA_W), y_ss.reshape(m, SSM_W),
                      w_out[l].astype(BF16), row(mix_post_g[l]))

        xf = _ffn(xf, row(ffn2_pre_g[l]), ffn2_w_gu, ffn2_w_down, row(ffn2_post_g[l]), l)
    return xf.reshape(b_, t_, d_)
```

```python
import functools
import math

import jax
import jax.numpy as jnp
from jax import lax
from jax.experimental import pallas as pl
from jax.experimental.pallas import tpu as pltpu

F32 = jnp.float32
BF16 = jnp.bfloat16
HIGHEST = lax.Precision.HIGHEST

D_MODEL = 2048
DEPTH = 2
RW_HEADS = 12
RW_HD = 64
RW_W = RW_HEADS * RW_HD
RW_DECAY_LORA = 64
RW_AAA_LORA = 64
RW_GATE_LORA = 128
RW_GN_EPS = 64e-5
DA_HEADS = 6
DA_HD = 64
DA_W = DA_HEADS * 2 * DA_HD
DA_SUBLN_EPS = 1e-5
SSM_W = D_MODEL - RW_W - DA_W
SSM_GROUP = 16
SSM_GROUPS = SSM_W // SSM_GROUP
SSM_STATE = 64
SSM_N = SSM_GROUPS * SSM_STATE
RW_COLS = 3 * RW_W + RW_DECAY_LORA + RW_AAA_LORA + RW_GATE_LORA
DA_COLS = 3 * DA_W
D_FF = 5504
NORM_EPS = 1e-6

LANES = 128
SUBLANES = 8
VMEM_LIMIT = 56 * 1024 * 1024
FF_TILE = 256
FFN_ROW_TILE = 1024
IN_ROW_TILE = 1024
IN_COLS = RW_COLS + DA_COLS + SSM_W
IN_TILE = IN_COLS // 3
SSM_COL0 = RW_COLS + DA_COLS
ROW_TILE = 512
RW_CHUNK = 64
RW_TSTEP = 256
RW_MAP_HEADS = 4
RW_MAP_TSTEP = 512
ATT_TILE = 512
SSM_TSTEP = 256
PREP_TSTEP = 256
NEG_BIG = -1e30


def _dot(a, b, precision=None):
    return jnp.dot(a, b, preferred_element_type=F32, precision=precision)


def _dot_nt(a, b, precision=None):
    return lax.dot_general(a, b, (((1,), (1,)), ((), ())),
                           preferred_element_type=F32, precision=precision)


def _rms(x, g, eps):
    return x * lax.rsqrt(jnp.mean(x * x, axis=-1, keepdims=True) + eps) * g


def _params(sem):
    return pltpu.CompilerParams(dimension_semantics=sem, vmem_limit_bytes=VMEM_LIMIT)


def _ff_offset(j, base=0):
    return (base // LANES + jnp.minimum(j * (FF_TILE // LANES), (D_FF - FF_TILE) // LANES)) * LANES


def _ffn_kernel(*refs, emit_next):
    if emit_next:
        x_ref, pre_ref, wg_ref, wu_ref, wd_ref, post_ref, ng_ref, o_ref, nx_ref, xn_ref = refs
    else:
        x_ref, pre_ref, wg_ref, wu_ref, wd_ref, post_ref, o_ref, xn_ref = refs
    j = pl.program_id(1)

    @pl.when(j == 0)
    def _():
        xn_ref[...] = _rms(x_ref[...], pre_ref[...], NORM_EPS).astype(BF16)
        o_ref[...] = jnp.zeros_like(o_ref)

    xn = xn_ref[...]
    gate = _dot(xn, wg_ref[0].astype(BF16))
    up = _dot(xn, wu_ref[0].astype(BF16))
    h = gate * jax.nn.sigmoid(gate) * up
    unit = _ff_offset(j) + lax.broadcasted_iota(jnp.int32, h.shape, 1)
    h = jnp.where(unit >= j * FF_TILE, h, 0.0).astype(BF16)
    o_ref[...] += _dot(h, wd_ref[0].astype(BF16))

    @pl.when(j == pl.num_programs(1) - 1)
    def _():
        y = x_ref[...] + 0.5 * _rms(o_ref[...], post_ref[...], NORM_EPS)
        o_ref[...] = y
        if emit_next:
            nx_ref[...] = _rms(y, ng_ref[...], NORM_EPS).astype(BF16)


def _ffn(x, pre_g, w_gu, w_down, post_g, layer, next_g=None):
    m, d = x.shape
    tm, tf = min(FFN_ROW_TILE, m), FF_TILE
    once = pl.Buffered(1)
    emit_next = next_g is not None
    vec = pl.BlockSpec((1, d), lambda i, j: (0, 0))
    row_out = pl.BlockSpec((tm, d), lambda i, j: (i, 0), pipeline_mode=once)
    return pl.pallas_call(
        functools.partial(_ffn_kernel, emit_next=emit_next),
        out_shape=[jax.ShapeDtypeStruct((m, d), F32)] + [jax.ShapeDtypeStruct((m, d), BF16)] * emit_next,
        grid=(m // tm, pl.cdiv(D_FF, tf)),
        in_specs=[
            pl.BlockSpec((tm, d), lambda i, j: (i, 0)),
            vec,
            pl.BlockSpec((pl.Element(1), pl.Element(d), pl.Element(tf)), lambda i, j: (layer, 0, _ff_offset(j))),
            pl.BlockSpec((pl.Element(1), pl.Element(d), pl.Element(tf)), lambda i, j: (layer, 0, _ff_offset(j, D_FF))),
            pl.BlockSpec((pl.Element(1), pl.Element(tf), pl.Element(d)), lambda i, j: (layer, _ff_offset(j), 0)),
            vec,
        ] + [vec] * emit_next,
        out_specs=[row_out] * (1 + emit_next),
        scratch_shapes=[pltpu.VMEM((tm, d), BF16)],
        compiler_params=_params(("parallel", "arbitrary")),
        name="ffn",
    )(x, pre_g, w_gu, w_gu, w_down, post_g, *([next_g] * emit_next))


def _inproj_kernel(xn_ref, w_ref, o_ref, wb_ref):
    @pl.when(pl.program_id(1) == 0)
    def _():
        wb_ref[...] = w_ref[0].astype(BF16)

    o_ref[...] = _dot(xn_ref[...], wb_ref[...])


def _inproj(xn, w_in, layer):
    m, d = xn.shape
    tm, tn = min(IN_ROW_TILE, m), IN_TILE
    return pl.pallas_call(
        _inproj_kernel,
        out_shape=jax.ShapeDtypeStruct((m, IN_COLS), F32),
        grid=(IN_COLS // tn, m // tm),
        in_specs=[
            pl.BlockSpec((tm, d), lambda j, i: (i, 0)),
            pl.BlockSpec((pl.Element(1), pl.Element(d), pl.Element(tn)), lambda j, i: (layer, 0, j * tn),
                         pipeline_mode=pl.Buffered(1)),
        ],
        out_specs=pl.BlockSpec((tm, tn), lambda j, i: (i, j)),
        scratch_shapes=[pltpu.VMEM((d, tn), BF16)],
        compiler_params=_params(("parallel", "arbitrary")),
        name="inproj",
    )(xn, w_in)


def _softplus(x):
    return jnp.maximum(x, 0.0) + jnp.log(1.0 + jnp.exp(-jnp.abs(x)))


def _rwprep_kernel(z_ref, zp_ref, mu_ref, w0_ref, w2_ref, a0_ref, a2_ref, g2_ref, kk_ref, ka_ref,
                   r_o, lw_o, k_o, v_o, kk_o, b_o, g_o):
    i = pl.program_id(1)
    z = z_ref[0]
    tt = z.shape[0]
    last_prev = jnp.where(i == 0, 0.0, zp_ref[0, SUBLANES - 1:SUBLANES, :])
    row = lax.broadcasted_iota(jnp.int32, z.shape, 0)
    z_prev = jnp.where(row == 0, last_prev, pltpu.roll(z, 1, 0))
    z = z + (z_prev - z) * mu_ref[...]
    o1, o2, o3 = RW_W, 2 * RW_W, 3 * RW_W
    o4, o5 = o3 + RW_DECAY_LORA, o3 + RW_DECAY_LORA + RW_AAA_LORA
    r, k, v = z[:, :o1], z[:, o1:o2], z[:, o2:o3]
    zw, za, zg = z[:, o3:o4], z[:, o4:o5], z[:, o5:]
    w = -_softplus(-(w0_ref[...] + _dot(jnp.tanh(zw), w2_ref[...], HIGHEST))) - 0.5
    lw = -jnp.exp(w)
    a = jax.nn.sigmoid(a0_ref[...] + _dot(za, a2_ref[...], HIGHEST))
    g = _dot(jax.nn.sigmoid(zg), g2_ref[...], HIGHEST)
    kk = k * kk_ref[...]
    k = k * (1.0 + (a - 1.0) * ka_ref[...])
    for h in range(RW_HEADS):
        sl = slice(h * RW_HD, (h + 1) * RW_HD)
        kk_h = kk[:, sl]
        nrm = jnp.sqrt(jnp.sum(kk_h * kk_h, axis=-1, keepdims=True))
        kk_h = kk_h / jnp.maximum(nrm, 1e-12)
        r_o[0, h] = r[:, sl]
        lw_o[0, h] = lw[:, sl]
        k_o[0, h] = k[:, sl]
        v_o[0, h] = v[:, sl]
        kk_o[0, h] = kk_h
        b_o[0, h] = kk_h * a[:, sl]
        g_o[0, h] = g[:, sl]


def _rwprep(z, mu, w0, w2, a0, a2, g2, k_k, k_a):
    b_, t_, _ = z.shape
    tt = min(PREP_TSTEP, t_)
    nprev = tt // SUBLANES
    full = lambda shape: pl.BlockSpec(shape, lambda b, i: (0,) * len(shape))
    head_out = jax.ShapeDtypeStruct((b_, RW_HEADS, t_, RW_HD), F32)
    head_spec = pl.BlockSpec((1, RW_HEADS, tt, RW_HD), lambda b, i: (b, 0, i, 0))
    return pl.pallas_call(
        _rwprep_kernel,
        out_shape=[head_out] * 7,
        grid=(b_, t_ // tt),
        in_specs=[
            pl.BlockSpec((1, tt, RW_COLS), lambda b, i: (b, i, 0)),
            pl.BlockSpec((1, SUBLANES, RW_COLS), lambda b, i: (b, jnp.maximum(i * nprev - 1, 0), 0)),
            full((1, RW_COLS)), full((1, RW_W)), full((RW_DECAY_LORA, RW_W)), full((1, RW_W)),
            full((RW_AAA_LORA, RW_W)), full((RW_GATE_LORA, RW_W)), full((1, RW_W)), full((1, RW_W)),
        ],
        out_specs=[head_spec] * 7,
        compiler_params=_params(("parallel", "arbitrary")),
        name="rwkv_features",
    )(z, z, mu, w0, w2, a0, a2, g2, k_k, k_a)


_NN = ((1,), (0,))
_NT = ((1,), (1,))
_TN = ((0,), (0,))


def _bdot(a, b, dims=_NN):
    return lax.dot_general(a.astype(BF16), b.astype(BF16), (dims, ((), ())), preferred_element_type=F32)


def _unit_lower_inverses(ns, row, col):
    c = ns[0].shape[0]
    eye = (row == col).astype(F32)
    base = SUBLANES
    diag = (row // base) == (col // base)
    a0 = [jnp.where(diag, n, 0.0) for n in ns]
    a2 = [_bdot(a, a) for a in a0]
    a4 = [_bdot(a, a) for a in a2]
    xs = [_bdot(eye + p, eye + q) for p, q in zip(a0, a2)]
    xs = [_bdot(p, eye + q) for p, q in zip(xs, a4)]
    m = base
    while m < c:
        off = ((row // (2 * m)) == (col // (2 * m))) & ((row // m) != (col // m))
        xe = [_bdot(p, jnp.where(off, n, 0.0)) for p, n in zip(xs, ns)]
        xs = [p + _bdot(q, p) for p, q in zip(xs, xe)]
        m *= 2
    return xs


def _rwchunk_kernel(r_ref, lw_ref, k_ref, v_ref, kk_ref, b_ref, rk_ref, rp_o, y0_o, tm_o, sadd_o, bonus_o):
    c = RW_CHUNK
    nheads, tstep, kd = r_ref.shape[1:]
    row = lax.broadcasted_iota(jnp.int32, (c, c), 0)
    col = lax.broadcasted_iota(jnp.int32, (c, c), 1)
    strict = row > col
    incl = row >= col
    tri = incl.astype(BF16)
    items = [(hh, pl.ds(ci * c, c)) for hh in range(nheads) for ci in range(tstep // c)]
    ld = lambda ref: [ref[0, hh, rows, :] for hh, rows in items]
    r, lw, k, v, kk, b = ld(r_ref), ld(lw_ref), ld(k_ref), ld(v_ref), ld(kk_ref), ld(b_ref)

    lw_hi = [x.astype(BF16) for x in lw]
    lc_hi = [_bdot(tri, h) for h in lw_hi]
    lc_lo = [_bdot(tri, x - h.astype(F32)) for x, h in zip(lw, lw_hi)]
    lc = [p + q for p, q in zip(lc_hi, lc_lo)]
    e_inc = [jnp.exp(x) for x in lc]
    e_inv = [jnp.exp(-x) for x in lc]
    e_end = [jnp.exp(x[c - 1:c, :] - x) for x in lc]
    at = [-q * jnp.exp(x - w) for q, x, w in zip(kk, lc, lw)]
    rt = [p * e for p, e in zip(r, e_inc)]
    m1 = [_bdot(jnp.concatenate([p, q], axis=0), jnp.concatenate([bb * e, kx * e], axis=0), _NT)
          for p, q, bb, kx, e in zip(at, rt, b, k, e_inv)]
    a_ab = [jnp.where(strict, m[:c, :c], 0.0) for m in m1]
    a_ak = [jnp.where(strict, m[:c, c:], 0.0) for m in m1]
    r_b = [jnp.where(incl, m[c:, :c], 0.0) for m in m1]
    r_k = [jnp.where(incl, m[c:, c:], 0.0) for m in m1]
    av = [_bdot(p, q) for p, q in zip(a_ak, v)]
    rkv = [_bdot(p, q) for p, q in zip(r_k, v)]
    kev = [_bdot(kx * e, q, _TN) for kx, e, q in zip(k, e_end, v)]
    xs = _unit_lower_inverses(a_ab, row, col)
    z = [_bdot(x, jnp.concatenate([p, q], axis=1)) for x, p, q in zip(xs, at, av)]
    rw = [_bdot(p, q) for p, q in zip(r_b, z)]
    tz = [_bdot(bb * e, q, _TN) for bb, e, q in zip(b, e_end, z)]
    eye_k = lax.broadcasted_iota(jnp.int32, (kd, kd), 0) == lax.broadcasted_iota(jnp.int32, (kd, kd), 1)
    for i, (hh, rows) in enumerate(items):
        rp_o[0, hh, rows, :] = rt[i] + rw[i][:, :kd]
        y0_o[0, hh, rows, :] = rw[i][:, kd:] + rkv[i]
        tm_o[0, hh, rows, :] = jnp.where(eye_k, jnp.broadcast_to(e_inc[i][c - 1:c, :], (kd, kd)), 0.0) + tz[i][:, :kd]
        sadd_o[0, hh, rows, :] = tz[i][:, kd:] + kev[i]
        bonus_o[0, hh, rows, :] = jnp.sum(r[i] * k[i] * rk_ref[hh], axis=-1, keepdims=True) * v[i]


def _rwchunk(r, lw, k, v, kk, b, r_k):
    b_, h_, t_, d_ = r.shape
    assert RW_CHUNK == d_
    hp = RW_MAP_HEADS
    tstep = min(RW_MAP_TSTEP, t_)
    seq = pl.BlockSpec((1, hp, tstep, d_), lambda bb, h, i: (bb, h, i, 0))
    par = pl.BlockSpec((hp, 1, d_), lambda bb, h, i: (h, 0, 0))
    return pl.pallas_call(
        _rwchunk_kernel,
        out_shape=[jax.ShapeDtypeStruct(r.shape, F32)] * 5,
        grid=(b_, h_ // hp, t_ // tstep),
        in_specs=[seq] * 6 + [par],
        out_specs=[seq] * 5,
        compiler_params=_params(("parallel", "parallel", "parallel")),
        name="rwkv_chunk",
    )(r, lw, k, v, kk, b, r_k)


def _rwstate_kernel(rp_ref, y0_ref, tm_ref, sadd_ref, bonus_ref, g_ref, gnw_ref, gnb_ref, o_ref, s_ref):
    @pl.when(pl.program_id(1) == 0)
    def _():
        s_ref[...] = jnp.zeros_like(s_ref)

    c = RW_CHUNK
    nheads, tstep, _ = rp_ref.shape[1:]
    heads = range(nheads)
    for ci in range(tstep // c):
        rows = pl.ds(ci * c, c)
        st = [s_ref[hh] for hh in heads]
        new = [_bdot(tm_ref[0, hh, rows, :], st[hh]) for hh in heads]
        ys = [_bdot(rp_ref[0, hh, rows, :], st[hh]) for hh in heads]
        outs = []
        for hh in heads:
            s_ref[hh] = new[hh] + sadd_ref[0, hh, rows, :]
            y = ys[hh] + y0_ref[0, hh, rows, :]
            mean = jnp.mean(y, axis=-1, keepdims=True)
            var = jnp.mean(jnp.square(y - mean), axis=-1, keepdims=True)
            yn = (y - mean) * lax.rsqrt(var + RW_GN_EPS) * gnw_ref[hh] + gnb_ref[hh]
            outs.append((yn + bonus_ref[0, hh, rows, :]) * g_ref[0, hh, rows, :])
        o_ref[0, rows, :] = jnp.concatenate(outs, axis=-1)


def _rwstate(rp, y0, tm, sadd, bonus, g, gn_w, gn_b):
    b_, h_, t_, d_ = rp.shape
    tstep = min(RW_TSTEP, t_)
    seq = pl.BlockSpec((1, h_, tstep, d_), lambda bb, i: (bb, 0, i, 0))
    par = pl.BlockSpec((h_, 1, d_), lambda bb, i: (0, 0, 0))
    return pl.pallas_call(
        _rwstate_kernel,
        out_shape=jax.ShapeDtypeStruct((b_, t_, h_ * d_), F32),
        grid=(b_, t_ // tstep),
        in_specs=[seq] * 6 + [par] * 2,
        out_specs=pl.BlockSpec((1, tstep, h_ * d_), lambda bb, i: (bb, i, 0)),
        scratch_shapes=[pltpu.VMEM((h_, d_, d_), F32)],
        compiler_params=_params(("parallel", "arbitrary")),
        name="rwkv_state",
    )(rp, y0, tm, sadd, bonus, g, gn_w, gn_b)


def _attn_kernel(q_ref, k_ref, v_ref, lq1_ref, lk1_ref, lq2_ref, lk2_ref, sw_ref, o_ref,
                 qs_ref, m_ref, acc_ref, *, lam_init):
    qi = pl.program_id(2)
    t = q_ref.shape[1]
    q = q_ref[0] * (DA_HD ** -0.5)
    lane = lax.broadcasted_iota(jnp.int32, q.shape, 1)
    qs_ref[0] = jnp.where(lane < DA_HD, q, 0.0).astype(BF16)
    qs_ref[1] = jnp.where(lane >= DA_HD, q, 0.0).astype(BF16)
    m_ref[...] = jnp.full_like(m_ref, NEG_BIG)
    acc_ref[...] = jnp.zeros_like(acc_ref)
    hw = q.shape[1]
    ones = jnp.ones((t, hw), BF16)
    subheads = range(2)

    def key_tiles(js, diagonal):
        kbs, vbs = [], []
        for j in js:
            rows = pl.ds(pl.multiple_of(j * t, t), t)
            kbs.append(k_ref[0, rows, :].astype(BF16))
            vbs.append(jnp.concatenate([v_ref[0, rows, :].astype(BF16), ones], axis=1))
        s = [[_dot_nt(qs_ref[c], kb) for kb in kbs] for c in subheads]
        if diagonal:
            causal = (lax.broadcasted_iota(jnp.int32, (t, t), 1) <= lax.broadcasted_iota(jnp.int32, (t, t), 0))
            for c in subheads:
                s[c][-1] = jnp.where(causal, s[c][-1], NEG_BIG)
        m_old = [m_ref[c] for c in subheads]
        m_new = []
        for c in subheads:
            m = m_old[c]
            for x in s[c]:
                m = jnp.maximum(m, jnp.max(x, axis=-1, keepdims=True))
            m_new.append(m)
        m_wide = [jnp.concatenate([m] * (t // hw), axis=1) for m in m_new]
        p = [[jnp.exp(x - m_wide[c]).astype(BF16) for x in s[c]] for c in subheads]
        pv = [sum(_dot(x, vb) for x, vb in zip(p[c], vbs)) for c in subheads]
        for c in subheads:
            alpha = jnp.exp(m_old[c] - m_new[c])
            acc_ref[c] = jnp.concatenate([alpha, alpha], axis=1) * acc_ref[c] + pv[c]
            m_ref[c] = m_new[c]

    def pair_below_diagonal(jj, carry):
        key_tiles([2 * jj, 2 * jj + 1], False)
        return carry

    lax.fori_loop(0, qi // 2, pair_below_diagonal, 0)

    @pl.when(qi % 2 == 1)
    def _():
        key_tiles([qi - 1], False)

    key_tiles([qi], True)

    lam = (jnp.exp(jnp.sum(lq1_ref[...] * lk1_ref[...], axis=-1, keepdims=True))
           - jnp.exp(jnp.sum(lq2_ref[...] * lk2_ref[...], axis=-1, keepdims=True)) + lam_init)
    o = acc_ref[0, :, :hw] / acc_ref[0, :, hw:] - lam * (acc_ref[1, :, :hw] / acc_ref[1, :, hw:])
    o = o * lax.rsqrt(jnp.mean(o * o, axis=-1, keepdims=True) + DA_SUBLN_EPS) * sw_ref[...]
    o_ref[0] = o * (1.0 - lam_init)


def _attn(z, lq1, lk1, lq2, lk2, subln_w, lam_init):
    b_, t_, _ = z.shape
    tq = min(ATT_TILE, t_)
    hw = 2 * DA_HD
    q0, k0, v0 = RW_COLS // hw, (RW_COLS + DA_W) // hw, (RW_COLS + 2 * DA_W) // hw
    small = lambda n: pl.BlockSpec((1, n), lambda b, h, qi: (0, 0))
    return pl.pallas_call(
        functools.partial(_attn_kernel, lam_init=lam_init),
        out_shape=jax.ShapeDtypeStruct((b_, t_, DA_W), F32),
        grid=(b_, DA_HEADS, t_ // tq),
        in_specs=[
            pl.BlockSpec((1, tq, hw), lambda b, h, qi: (b, qi, q0 + h)),
            pl.BlockSpec((1, t_, hw), lambda b, h, qi: (b, 0, k0 + h)),
            pl.BlockSpec((1, t_, hw), lambda b, h, qi: (b, 0, v0 + h)),
            small(DA_HD), small(DA_HD), small(DA_HD), small(DA_HD), small(hw),
        ],
        out_specs=pl.BlockSpec((1, tq, hw), lambda b, h, qi: (b, qi, h)),
        scratch_shapes=[pltpu.VMEM((2, tq, hw), BF16), pltpu.VMEM((2, tq, hw), F32),
                        pltpu.VMEM((2, tq, 2 * hw), F32)],
        compiler_params=_params(("parallel", "parallel", "arbitrary")),
        name="diff_attn",
    )(z, z, z, lq1, lk1, lq2, lk2, subln_w)


def _gelu_tanh(x):
    return 0.5 * x * (1.0 + jnp.tanh(math.sqrt(2.0 / math.pi) * (x + 0.044715 * (x * x * x))))


def _s5_kernel(u_ref, bmat_ref, pw_ref, step_ref, cmat_ref, d_ref, wglu_ref, bglu_ref, o_ref,
               xs_ref, carry_ref):
    @pl.when(pl.program_id(1) == 0)
    def _():
        carry_ref[...] = jnp.zeros_like(carry_ref)

    u = u_ref[0]
    tt = u.shape[0]
    n = SSM_N
    hw, hn = SSM_W // 2, n // 2
    ub = u.astype(BF16)
    for h in range(2):
        ch = slice(h * hw, (h + 1) * hw)
        for part in (0, n):
            st = slice(part + h * hn, part + (h + 1) * hn)
            xs_ref[:, st] = _dot(ub[:, ch], bmat_ref[ch, st])
    pr, pi = pw_ref[0], pw_ref[1]

    def block(i, carry):
        cr, ci = carry
        rows = pl.ds(pl.multiple_of(i * SUBLANES, SUBLANES), SUBLANES)
        xr = xs_ref[rows, :n]
        xi = xs_ref[rows, n:]
        for lvl, d in enumerate((1, 2, 4)):
            ar, ai = step_ref[lvl, :, :n], step_ref[lvl, :, n:]
            sr, si = pltpu.roll(xr, d, 0), pltpu.roll(xi, d, 0)
            xr, xi = xr + ar * sr - ai * si, xi + ar * si + ai * sr
        xr, xi = xr + pr * cr - pi * ci, xi + pr * ci + pi * cr
        xs_ref[rows, :n] = xr
        xs_ref[rows, n:] = xi
        return xr[SUBLANES - 1:, :], xi[SUBLANES - 1:, :]

    cr, ci = lax.fori_loop(0, tt // SUBLANES, block, (carry_ref[0:1, :], carry_ref[1:2, :]))
    carry_ref[0:1, :] = cr
    carry_ref[1:2, :] = ci
    ys = []
    for h in range(2):
        ch = slice(h * hw, (h + 1) * hw)
        re, im = slice(h * hn, (h + 1) * hn), slice(n + h * hn, n + (h + 1) * hn)
        ys.append(_dot(xs_ref[:, re].astype(BF16), cmat_ref[re, ch]) + _dot(xs_ref[:, im].astype(BF16), cmat_ref[im, ch]))
    y = jnp.concatenate(ys, axis=1) + d_ref[...] * u
    y = _gelu_tanh(y)
    o_ref[0] = y * jax.nn.sigmoid(_dot(y.astype(BF16), wglu_ref[...]) + bglu_ref[...])


def _s5(z, bmat, pw, step, cmat, d_skip, w_glu, b_glu):
    b_, t_, _ = z.shape
    tt = min(SSM_TSTEP, t_)
    n = SSM_N
    full = lambda shape: pl.BlockSpec(shape, lambda b, i: (0,) * len(shape))
    return pl.pallas_call(
        _s5_kernel,
        out_shape=jax.ShapeDtypeStruct((b_, t_, SSM_W), F32),
        grid=(b_, t_ // tt),
        in_specs=[
            pl.BlockSpec((pl.Element(1), pl.Element(tt), pl.Element(SSM_W)), lambda b, i: (b, i * tt, SSM_COL0)),
            full((SSM_W, 2 * n)), full((2, SUBLANES, n)), full((3, SUBLANES, 2 * n)), full((2 * n, SSM_W)),
            full((1, SSM_W)), full((SSM_W, SSM_W)), full((1, SSM_W)),
        ],
        out_specs=pl.BlockSpec((1, tt, SSM_W), lambda b, i: (b, i, 0)),
        scratch_shapes=[pltpu.VMEM((tt, 2 * n), F32), pltpu.VMEM((2, n), F32)],
        compiler_params=_params(("parallel", "arbitrary")),
        name="s5",
    )(z, bmat, pw, step, cmat, d_skip, w_glu, b_glu)


def _s5_discretise(a_re, a_im, log_dt, b_re, b_im, c_re, c_im):
    g_, n_, c_ = SSM_GROUPS, SSM_STATE, SSM_GROUP
    dt = jnp.exp(log_dt)[:, None]
    mag = jnp.exp(dt * a_re)
    abar_r, abar_i = mag * jnp.cos(dt * a_im), mag * jnp.sin(dt * a_im)
    den = a_re * a_re + a_im * a_im
    nr, ni = abar_r - 1.0, abar_i
    coef_r, coef_i = (nr * a_re + ni * a_im) / den, (ni * a_re - nr * a_im) / den
    bbar_r = coef_r[..., None] * b_re - coef_i[..., None] * b_im
    bbar_i = coef_r[..., None] * b_im + coef_i[..., None] * b_re
    eye = jnp.eye(g_, dtype=F32)
    bd_in = lambda m: jnp.einsum('gnc,gh->gchn', m, eye).reshape(g_ * c_, g_ * n_)
    bmat = jnp.concatenate([bd_in(bbar_r), bd_in(bbar_i)], axis=1).astype(BF16)
    bd_out = lambda m: jnp.einsum('gcn,gh->gnhc', m, eye).reshape(g_ * n_, g_ * c_)
    cmat = jnp.concatenate([bd_out(c_re), -bd_out(c_im)], axis=0).astype(BF16)
    ar, ai = abar_r.reshape(1, -1), abar_i.reshape(1, -1)
    pows_r, pows_i = [ar], [ai]
    for _ in range(SUBLANES - 1):
        pr, pi = pows_r[-1], pows_i[-1]
        pows_r.append(pr * ar - pi * ai)
        pows_i.append(pr * ai + pi * ar)
    pw = jnp.stack([jnp.concatenate(pows_r, axis=0), jnp.concatenate(pows_i, axis=0)])
    t_idx = jnp.arange(SUBLANES)[:, None]
    step = jnp.stack([jnp.where(t_idx >= d, jnp.concatenate([pows_r[d - 1], pows_i[d - 1]], axis=1), 0.0)
                      for d in (1, 2, 4)])
    return bmat, pw, step, cmat


def _outproj_kernel(x_ref, yrw_ref, yda_ref, yss_ref, w1_ref, w2_ref, w3_ref, g_ref, o_ref):
    y = (_dot(yrw_ref[...].astype(BF16), w1_ref[...]) + _dot(yda_ref[...].astype(BF16), w2_ref[...])
         + _dot(yss_ref[...].astype(BF16), w3_ref[...]))
    o_ref[...] = x_ref[...] + _rms(y, g_ref[...], NORM_EPS)


def _outproj(x, y_rw, y_da, y_ss, w, g):
    m, d = x.shape
    tm = min(ROW_TILE, m)
    rowblk = lambda n: pl.BlockSpec((tm, n), lambda i: (i, 0))
    assert RW_W == DA_W and (RW_W + DA_W) % SSM_W == 0
    band = lambda rows, idx: pl.BlockSpec((rows, d), lambda i: (idx, 0))
    return pl.pallas_call(
        _outproj_kernel,
        out_shape=jax.ShapeDtypeStruct((m, d), F32),
        grid=(m // tm,),
        in_specs=[rowblk(d), rowblk(RW_W), rowblk(DA_W), rowblk(SSM_W),
                  band(RW_W, 0), band(DA_W, 1), band(SSM_W, (RW_W + DA_W) // SSM_W),
                  pl.BlockSpec((1, d), lambda i: (0, 0))],
        out_specs=rowblk(d),
        compiler_params=_params(("parallel",)),
        name="outproj",
    )(x, y_rw, y_da, y_ss, w, w, w, g)


def kernel(x, ffn1_pre_g, ffn1_w_gu, ffn1_w_down, ffn1_post_g, mix_pre_g, w_in, rw_mu, rw_w0, rw_w2, rw_a0, rw_a2, rw_g2, rw_k_k, rw_k_a, rw_r_k, rw_gn_w, rw_gn_b, da_lq1, da_lk1, da_lq2, da_lk2, da_subln_w, ssm_a_re, ssm_a_im, ssm_log_dt, ssm_b_re, ssm_b_im, ssm_c_re, ssm_c_im, ssm_d, ssm_w_glu, ssm_b_glu, w_out, mix_post_g, ffn2_pre_g, ffn2_w_gu, ffn2_w_down, ffn2_post_g):
    b_, t_, d_ = x.shape
    m = b_ * t_
    row = lambda a: a.reshape(1, -1)
    xf = x.reshape(m, d_)
    for l in range(DEPTH):
        xf, xn = _ffn(xf, row(ffn1_pre_g[l]), ffn1_w_gu, ffn1_w_down, row(ffn1_post_g[l]), l, row(mix_pre_g[l]))

        z = _inproj(xn, w_in, l).reshape(b_, t_, IN_COLS)

        feats = _rwprep(z, row(rw_mu[l]), row(rw_w0[l]), rw_w2[l], row(rw_a0[l]), rw_a2[l], rw_g2[l],
                        row(rw_k_k[l]), row(rw_k_a[l]))
        per_head = lambda a: a.reshape(RW_HEADS, 1, RW_HD)
        r, lw, k, v, kk, b, g = feats
        maps = _rwchunk(r, lw, k, v, kk, b, per_head(rw_r_k[l]))
        y_rw = _rwstate(*maps, g, per_head(rw_gn_w[l]), per_head(rw_gn_b[l]))

        lam_init = 0.8 - 0.6 * math.exp(-0.3 * l)
        y_da = _attn(z, row(da_lq1[l]), row(da_lk1[l]), row(da_lq2[l]), row(da_lk2[l]),
                     row(da_subln_w[l]), lam_init)

        bmat, pw, step, cmat = _s5_discretise(ssm_a_re[l], ssm_a_im[l], ssm_log_dt[l], ssm_b_re[l],
                                              ssm_b_im[l], ssm_c_re[l], ssm_c_im[l])
        y_ss = _s5(z, bmat, pw, step, cmat, row(ssm_d[l]), ssm_w_glu[l].astype(BF16), row(ssm_b_glu[l]))

        xf = _outproj(xf, y_rw.reshape(m, RW_W), y_da.reshape(m, DA_W), y_ss.reshape(m, SSM_W),
                      w_out[l].astype(BF16), row(mix_post_g[l]))

        xf, = _ffn(xf, row(ffn2_pre_g[l]), ffn2_w_gu, ffn2_w_down, row(ffn2_post_g[l]), l)
    return xf.reshape(b_, t_, d_)
```

```python
import functools
import math

import jax
import jax.numpy as jnp
from jax import lax
from jax.experimental import pallas as pl
from jax.experimental.pallas import tpu as pltpu

F32 = jnp.float32
BF16 = jnp.bfloat16
HIGHEST = lax.Precision.HIGHEST

D_MODEL = 2048
DEPTH = 2
RW_HEADS = 12
RW_HD = 64
RW_W = RW_HEADS * RW_HD
RW_DECAY_LORA = 64
RW_AAA_LORA = 64
RW_GATE_LORA = 128
RW_GN_EPS = 64e-5
DA_HEADS = 6
DA_HD = 64
DA_W = DA_HEADS * 2 * DA_HD
DA_SUBLN_EPS = 1e-5
SSM_W = D_MODEL - RW_W - DA_W
SSM_GROUP = 16
SSM_GROUPS = SSM_W // SSM_GROUP
SSM_STATE = 64
SSM_N = SSM_GROUPS * SSM_STATE
RW_COLS = 3 * RW_W + RW_DECAY_LORA + RW_AAA_LORA + RW_GATE_LORA
DA_COLS = 3 * DA_W
D_FF = 5504
NORM_EPS = 1e-6

LANES = 128
SUBLANES = 8
VMEM_LIMIT = 56 * 1024 * 1024
FF_TILE = 256
FFN_ROW_TILE = 1024
IN_ROW_TILE = 1024
IN_COLS = RW_COLS + DA_COLS + SSM_W
IN_TILE = IN_COLS // 3
SSM_COL0 = RW_COLS + DA_COLS
ROW_TILE = 512
RW_CHUNK = 64
RW_TSTEP = 256
RW_STATE_HEADS = 4
RW_MAP_HEADS = 4
RW_MAP_TSTEP = 512
ATT_TILE = 512
SSM_TSTEP = 256
PREP_TSTEP = 256
NEG_BIG = -1e30


def _dot(a, b, precision=None):
    return jnp.dot(a, b, preferred_element_type=F32, precision=precision)


def _dot_nt(a, b, precision=None):
    return lax.dot_general(a, b, (((1,), (1,)), ((), ())),
                           preferred_element_type=F32, precision=precision)


def _rms(x, g, eps):
    return x * lax.rsqrt(jnp.mean(x * x, axis=-1, keepdims=True) + eps) * g


def _params(sem):
    return pltpu.CompilerParams(dimension_semantics=sem, vmem_limit_bytes=VMEM_LIMIT)


def _ff_offset(j, base=0):
    return (base // LANES + jnp.minimum(j * (FF_TILE // LANES), (D_FF - FF_TILE) // LANES)) * LANES


def _ffn_kernel(*refs, emit_next):
    if emit_next:
        x_ref, pre_ref, wg_ref, wu_ref, wd_ref, post_ref, ng_ref, o_ref, nx_ref, xn_ref = refs
    else:
        x_ref, pre_ref, wg_ref, wu_ref, wd_ref, post_ref, o_ref, xn_ref = refs
    j = pl.program_id(1)

    @pl.when(j == 0)
    def _():
        xn_ref[...] = _rms(x_ref[...], pre_ref[...], NORM_EPS).astype(BF16)
        o_ref[...] = jnp.zeros_like(o_ref)

    xn = xn_ref[...]
    gate = _dot(xn, wg_ref[0].astype(BF16))
    up = _dot(xn, wu_ref[0].astype(BF16))
    h = gate * jax.nn.sigmoid(gate) * up
    unit = _ff_offset(j) + lax.broadcasted_iota(jnp.int32, h.shape, 1)
    h = jnp.where(unit >= j * FF_TILE, h, 0.0).astype(BF16)
    o_ref[...] += _dot(h, wd_ref[0].astype(BF16))

    @pl.when(j == pl.num_programs(1) - 1)
    def _():
        y = x_ref[...] + 0.5 * _rms(o_ref[...], post_ref[...], NORM_EPS)
        o_ref[...] = y
        if emit_next:
            nx_ref[...] = _rms(y, ng_ref[...], NORM_EPS).astype(BF16)


def _ffn(x, pre_g, w_gu, w_down, post_g, layer, next_g=None):
    m, d = x.shape
    tm, tf = min(FFN_ROW_TILE, m), FF_TILE
    once = pl.Buffered(1)
    emit_next = next_g is not None
    vec = pl.BlockSpec((1, d), lambda i, j: (0, 0))
    row_out = pl.BlockSpec((tm, d), lambda i, j: (i, 0), pipeline_mode=once)
    return pl.pallas_call(
        functools.partial(_ffn_kernel, emit_next=emit_next),
        out_shape=[jax.ShapeDtypeStruct((m, d), F32)] + [jax.ShapeDtypeStruct((m, d), BF16)] * emit_next,
        grid=(m // tm, pl.cdiv(D_FF, tf)),
        in_specs=[
            pl.BlockSpec((tm, d), lambda i, j: (i, 0)),
            vec,
            pl.BlockSpec((pl.Element(1), pl.Element(d), pl.Element(tf)), lambda i, j: (layer, 0, _ff_offset(j))),
            pl.BlockSpec((pl.Element(1), pl.Element(d), pl.Element(tf)), lambda i, j: (layer, 0, _ff_offset(j, D_FF))),
            pl.BlockSpec((pl.Element(1), pl.Element(tf), pl.Element(d)), lambda i, j: (layer, _ff_offset(j), 0)),
            vec,
        ] + [vec] * emit_next,
        out_specs=[row_out] * (1 + emit_next),
        scratch_shapes=[pltpu.VMEM((tm, d), BF16)],
        compiler_params=_params(("parallel", "arbitrary")),
        name="ffn",
    )(x, pre_g, w_gu, w_gu, w_down, post_g, *([next_g] * emit_next))


def _inproj_kernel(xn_ref, w_ref, o_ref, wb_ref):
    @pl.when(pl.program_id(1) == 0)
    def _():
        wb_ref[...] = w_ref[0].astype(BF16)

    o_ref[...] = _dot(xn_ref[...], wb_ref[...])


def _inproj(xn, w_in, layer):
    m, d = xn.shape
    tm, tn = min(IN_ROW_TILE, m), IN_TILE
    return pl.pallas_call(
        _inproj_kernel,
        out_shape=jax.ShapeDtypeStruct((m, IN_COLS), F32),
        grid=(IN_COLS // tn, m // tm),
        in_specs=[
            pl.BlockSpec((tm, d), lambda j, i: (i, 0)),
            pl.BlockSpec((pl.Element(1), pl.Element(d), pl.Element(tn)), lambda j, i: (layer, 0, j * tn),
                         pipeline_mode=pl.Buffered(1)),
        ],
        out_specs=pl.BlockSpec((tm, tn), lambda j, i: (i, j)),
        scratch_shapes=[pltpu.VMEM((d, tn), BF16)],
        compiler_params=_params(("parallel", "arbitrary")),
        name="inproj",
    )(xn, w_in)


def _softplus(x):
    return jnp.maximum(x, 0.0) + jnp.log(1.0 + jnp.exp(-jnp.abs(x)))


def _rwprep_kernel(z_ref, zp_ref, mu_ref, w0_ref, w2_ref, a0_ref, a2_ref, g2_ref, kk_ref, ka_ref,
                   r_o, lw_o, k_o, v_o, kk_o, b_o, g_o):
    i = pl.program_id(1)
    z = z_ref[0]
    tt = z.shape[0]
    last_prev = jnp.where(i == 0, 0.0, zp_ref[0, SUBLANES - 1:SUBLANES, :])
    row = lax.broadcasted_iota(jnp.int32, z.shape, 0)
    z_prev = jnp.where(row == 0, last_prev, pltpu.roll(z, 1, 0))
    z = z + (z_prev - z) * mu_ref[...]
    o1, o2, o3 = RW_W, 2 * RW_W, 3 * RW_W
    o4, o5 = o3 + RW_DECAY_LORA, o3 + RW_DECAY_LORA + RW_AAA_LORA
    r, k, v = z[:, :o1], z[:, o1:o2], z[:, o2:o3]
    zw, za, zg = z[:, o3:o4], z[:, o4:o5], z[:, o5:]
    w = -_softplus(-(w0_ref[...] + _dot(jnp.tanh(zw), w2_ref[...], HIGHEST))) - 0.5
    lw = -jnp.exp(w)
    a = jax.nn.sigmoid(a0_ref[...] + _dot(za, a2_ref[...], HIGHEST))
    g = _dot(jax.nn.sigmoid(zg), g2_ref[...], HIGHEST)
    kk = k * kk_ref[...]
    k = k * (1.0 + (a - 1.0) * ka_ref[...])
    for h in range(RW_HEADS):
        sl = slice(h * RW_HD, (h + 1) * RW_HD)
        kk_h = kk[:, sl]
        nrm = jnp.sqrt(jnp.sum(kk_h * kk_h, axis=-1, keepdims=True))
        kk_h = kk_h / jnp.maximum(nrm, 1e-12)
        r_o[0, h] = r[:, sl]
        lw_o[0, h] = lw[:, sl]
        k_o[0, h] = k[:, sl]
        v_o[0, h] = v[:, sl]
        kk_o[0, h] = kk_h
        b_o[0, h] = kk_h * a[:, sl]
    g_o[0] = g


def _rwprep(z, mu, w0, w2, a0, a2, g2, k_k, k_a):
    b_, t_, _ = z.shape
    tt = min(PREP_TSTEP, t_)
    nprev = tt // SUBLANES
    full = lambda shape: pl.BlockSpec(shape, lambda b, i: (0,) * len(shape))
    head_out = jax.ShapeDtypeStruct((b_, RW_HEADS, t_, RW_HD), F32)
    head_spec = pl.BlockSpec((1, RW_HEADS, tt, RW_HD), lambda b, i: (b, 0, i, 0))
    return pl.pallas_call(
        _rwprep_kernel,
        out_shape=[head_out] * 6 + [jax.ShapeDtypeStruct((b_, t_, RW_W), F32)],
        grid=(b_, t_ // tt),
        in_specs=[
            pl.BlockSpec((1, tt, RW_COLS), lambda b, i: (b, i, 0)),
            pl.BlockSpec((1, SUBLANES, RW_COLS), lambda b, i: (b, jnp.maximum(i * nprev - 1, 0), 0)),
            full((1, RW_COLS)), full((1, RW_W)), full((RW_DECAY_LORA, RW_W)), full((1, RW_W)),
            full((RW_AAA_LORA, RW_W)), full((RW_GATE_LORA, RW_W)), full((1, RW_W)), full((1, RW_W)),
        ],
        out_specs=[head_spec] * 6 + [pl.BlockSpec((1, tt, RW_W), lambda b, i: (b, i, 0))],
        compiler_params=_params(("parallel", "arbitrary")),
        name="rwkv_features",
    )(z, z, mu, w0, w2, a0, a2, g2, k_k, k_a)


_NN = ((1,), (0,))
_NT = ((1,), (1,))
_TN = ((0,), (0,))


def _bdot(a, b, dims=_NN):
    return lax.dot_general(a.astype(BF16), b.astype(BF16), (dims, ((), ())), preferred_element_type=F32)


def _unit_lower_inverses(ns, row, col):
    c = ns[0].shape[0]
    eye = (row == col).astype(F32)
    base = SUBLANES
    diag = (row // base) == (col // base)
    a0 = [jnp.where(diag, n, 0.0) for n in ns]
    a2 = [_bdot(a, a) for a in a0]
    a4 = [_bdot(a, a) for a in a2]
    xs = [_bdot(eye + p, eye + q) for p, q in zip(a0, a2)]
    xs = [_bdot(p, eye + q) for p, q in zip(xs, a4)]
    m = base
    while m < c:
        off = ((row // (2 * m)) == (col // (2 * m))) & ((row // m) != (col // m))
        xe = [_bdot(p, jnp.where(off, n, 0.0)) for p, n in zip(xs, ns)]
        xs = [p + _bdot(q, p) for p, q in zip(xs, xe)]
        m *= 2
    return xs


def _rwchunk_kernel(r_ref, lw_ref, k_ref, v_ref, kk_ref, b_ref, rk_ref, rp_o, y0_o, tm_o, sadd_o, bonus_o):
    c = RW_CHUNK
    nheads, tstep, kd = r_ref.shape[1:]
    row = lax.broadcasted_iota(jnp.int32, (c, c), 0)
    col = lax.broadcasted_iota(jnp.int32, (c, c), 1)
    strict = row > col
    incl = row >= col
    tri = incl.astype(BF16)
    items = [(hh, pl.ds(ci * c, c)) for hh in range(nheads) for ci in range(tstep // c)]
    ld = lambda ref: [ref[0, hh, rows, :] for hh, rows in items]
    r, lw, k, v, kk, b = ld(r_ref), ld(lw_ref), ld(k_ref), ld(v_ref), ld(kk_ref), ld(b_ref)

    lw_hi = [x.astype(BF16) for x in lw]
    lc_hi = [_bdot(tri, h) for h in lw_hi]
    lc_lo = [_bdot(tri, x - h.astype(F32)) for x, h in zip(lw, lw_hi)]
    lc = [p + q for p, q in zip(lc_hi, lc_lo)]
    e_inc = [jnp.exp(x) for x in lc]
    e_inv = [jnp.exp(-x) for x in lc]
    e_end = [jnp.exp(x[c - 1:c, :] - x) for x in lc]
    at = [-q * jnp.exp(x - w) for q, x, w in zip(kk, lc, lw)]
    rt = [p * e for p, e in zip(r, e_inc)]
    m1 = [_bdot(jnp.concatenate([p, q], axis=0), jnp.concatenate([bb * e, kx * e], axis=0), _NT)
          for p, q, bb, kx, e in zip(at, rt, b, k, e_inv)]
    a_ab = [jnp.where(strict, m[:c, :c], 0.0) for m in m1]
    a_ak = [jnp.where(strict, m[:c, c:], 0.0) for m in m1]
    r_b = [jnp.where(incl, m[c:, :c], 0.0) for m in m1]
    r_k = [jnp.where(incl, m[c:, c:], 0.0) for m in m1]
    av = [_bdot(p, q) for p, q in zip(a_ak, v)]
    rkv = [_bdot(p, q) for p, q in zip(r_k, v)]
    kev = [_bdot(kx * e, q, _TN) for kx, e, q in zip(k, e_end, v)]
    xs = _unit_lower_inverses(a_ab, row, col)
    z = [_bdot(x, jnp.concatenate([p, q], axis=1)) for x, p, q in zip(xs, at, av)]
    rw = [_bdot(p, q) for p, q in zip(r_b, z)]
    tz = [_bdot(bb * e, q, _TN) for bb, e, q in zip(b, e_end, z)]
    eye_k = lax.broadcasted_iota(jnp.int32, (kd, kd), 0) == lax.broadcasted_iota(jnp.int32, (kd, kd), 1)
    nchunks = tstep // c
    for ci in range(nchunks):
        rows = pl.ds(ci * c, c)
        ids = [hh * nchunks + ci for hh in range(nheads)]
        side_by_side = lambda pieces: jnp.concatenate(pieces, axis=1)
        rp_o[0, rows, :] = side_by_side([rt[i] + rw[i][:, :kd] for i in ids])
        y0_o[0, rows, :] = side_by_side([rw[i][:, kd:] + rkv[i] for i in ids])
        tm_o[0, rows, :] = side_by_side(
            [jnp.where(eye_k, jnp.broadcast_to(e_inc[i][c - 1:c, :], (kd, kd)), 0.0) + tz[i][:, :kd] for i in ids])
        sadd_o[0, rows, :] = side_by_side([tz[i][:, kd:] + kev[i] for i in ids])
        bonus_o[0, rows, :] = side_by_side(
            [jnp.sum(r[i] * k[i] * rk_ref[hh], axis=-1, keepdims=True) * v[i] for hh, i in enumerate(ids)])


def _rwchunk(r, lw, k, v, kk, b, r_k):
    b_, h_, t_, d_ = r.shape
    assert RW_CHUNK == d_
    hp = RW_MAP_HEADS
    tstep = min(RW_MAP_TSTEP, t_)
    seq = pl.BlockSpec((1, hp, tstep, d_), lambda bb, h, i: (bb, h, i, 0))
    par = pl.BlockSpec((hp, 1, d_), lambda bb, h, i: (h, 0, 0))
    return pl.pallas_call(
        _rwchunk_kernel,
        out_shape=[jax.ShapeDtypeStruct((b_, t_, h_ * d_), F32)] * 5,
        grid=(b_, h_ // hp, t_ // tstep),
        in_specs=[seq] * 6 + [par],
        out_specs=[pl.BlockSpec((1, tstep, hp * d_), lambda bb, h, i: (bb, i, h))] * 5,
        compiler_params=_params(("parallel", "parallel", "parallel")),
        name="rwkv_chunk",
    )(r, lw, k, v, kk, b, r_k)


def _rwstate_kernel(rp_ref, y0_ref, tm_ref, sadd_ref, bonus_ref, g_ref, gnw_ref, gnb_ref, o_ref, s_ref, y_ref):
    @pl.when(pl.program_id(1) == 0)
    def _():
        s_ref[...] = jnp.zeros_like(s_ref)

    c, kd = RW_CHUNK, RW_HD
    tstep, width = y_ref.shape
    gw = RW_STATE_HEADS * kd
    own_block = (lax.broadcasted_iota(jnp.int32, (gw, gw), 0) // kd) == (lax.broadcasted_iota(jnp.int32, (gw, gw), 1) // kd)
    for ci in range(tstep // c):
        rows = pl.ds(ci * c, c)
        for gi in range(width // gw):
            lanes = slice(gi * gw, (gi + 1) * gw)
            st = s_ref[:, lanes].astype(BF16)
            st_bd = jnp.where(own_block, jnp.concatenate([st] * RW_STATE_HEADS, axis=0), 0.0)
            lhs = jnp.concatenate([tm_ref[0, rows, lanes], rp_ref[0, rows, lanes]], axis=0).astype(BF16)
            res = _dot(lhs, st_bd)
            s_ref[:, lanes] = res[:c] + sadd_ref[0, rows, lanes]
            y_ref[rows, lanes] = res[c:] + y0_ref[0, rows, lanes]

    lane = lax.broadcasted_iota(jnp.int32, (tstep, LANES), 1)
    first = lane < kd

    def head_mean(x):
        s_first = jnp.sum(jnp.where(first, x, 0.0), axis=-1, keepdims=True)
        s_all = jnp.sum(x, axis=-1, keepdims=True)
        return jnp.where(first, s_first, s_all - s_first) * (1.0 / kd)

    for j in range(width // LANES):
        lanes = slice(j * LANES, (j + 1) * LANES)
        y = y_ref[:, lanes]
        d = y - head_mean(y)
        yn = d * lax.rsqrt(head_mean(d * d) + RW_GN_EPS) * gnw_ref[:, lanes] + gnb_ref[:, lanes]
        o_ref[0, :, lanes] = (yn + bonus_ref[0, :, lanes]) * g_ref[0, :, lanes]


def _rwstate(rp, y0, tm, sadd, bonus, g, gn_w, gn_b):
    b_, t_, w_ = rp.shape
    tstep = min(RW_TSTEP, t_)
    seq = pl.BlockSpec((1, tstep, w_), lambda bb, i: (bb, i, 0))
    par = pl.BlockSpec((1, w_), lambda bb, i: (0, 0))
    return pl.pallas_call(
        _rwstate_kernel,
        out_shape=jax.ShapeDtypeStruct((b_, t_, w_), F32),
        grid=(b_, t_ // tstep),
        in_specs=[seq] * 6 + [par] * 2,
        out_specs=seq,
        scratch_shapes=[pltpu.VMEM((RW_HD, w_), F32), pltpu.VMEM((tstep, w_), F32)],
        compiler_params=_params(("parallel", "arbitrary")),
        name="rwkv_state",
    )(rp, y0, tm, sadd, bonus, g, gn_w, gn_b)


def _attn_kernel(q_ref, k_ref, v_ref, lq1_ref, lk1_ref, lq2_ref, lk2_ref, sw_ref, o_ref,
                 qs_ref, m_ref, acc_ref, *, lam_init):
    qi = pl.program_id(2)
    t = q_ref.shape[1]
    q = q_ref[0] * (DA_HD ** -0.5)
    lane = lax.broadcasted_iota(jnp.int32, q.shape, 1)
    qs_ref[0] = jnp.where(lane < DA_HD, q, 0.0).astype(BF16)
    qs_ref[1] = jnp.where(lane >= DA_HD, q, 0.0).astype(BF16)
    m_ref[...] = jnp.full_like(m_ref, NEG_BIG)
    acc_ref[...] = jnp.zeros_like(acc_ref)
    hw = q.shape[1]
    ones = jnp.ones((t, hw), BF16)
    subheads = range(2)

    def key_tiles(js, diagonal):
        kbs, vbs = [], []
        for j in js:
            rows = pl.ds(pl.multiple_of(j * t, t), t)
            kbs.append(k_ref[0, rows, :].astype(BF16))
            vbs.append(jnp.concatenate([v_ref[0, rows, :].astype(BF16), ones], axis=1))
        s = [[_dot_nt(qs_ref[c], kb) for kb in kbs] for c in subheads]
        if diagonal:
            causal = (lax.broadcasted_iota(jnp.int32, (t, t), 1) <= lax.broadcasted_iota(jnp.int32, (t, t), 0))
            for c in subheads:
                s[c][-1] = jnp.where(causal, s[c][-1], NEG_BIG)
        m_old = [m_ref[c] for c in subheads]
        m_new = []
        for c in subheads:
            m = m_old[c]
            for x in s[c]:
                m = jnp.maximum(m, jnp.max(x, axis=-1, keepdims=True))
            m_new.append(m)
        m_wide = [jnp.concatenate([m] * (t // hw), axis=1) for m in m_new]
        p = [[jnp.exp(x - m_wide[c]).astype(BF16) for x in s[c]] for c in subheads]
        pv = [sum(_dot(x, vb) for x, vb in zip(p[c], vbs)) for c in subheads]
        for c in subheads:
            alpha = jnp.exp(m_old[c] - m_new[c])
            acc_ref[c] = jnp.concatenate([alpha, alpha], axis=1) * acc_ref[c] + pv[c]
            m_ref[c] = m_new[c]

    def pair_below_diagonal(jj, carry):
        key_tiles([2 * jj, 2 * jj + 1], False)
        return carry

    lax.fori_loop(0, qi // 2, pair_below_diagonal, 0)

    @pl.when(qi % 2 == 1)
    def _():
        key_tiles([qi - 1], False)

    key_tiles([qi], True)

    lam = (jnp.exp(jnp.sum(lq1_ref[...] * lk1_ref[...], axis=-1, keepdims=True))
           - jnp.exp(jnp.sum(lq2_ref[...] * lk2_ref[...], axis=-1, keepdims=True)) + lam_init)
    o = acc_ref[0, :, :hw] / acc_ref[0, :, hw:] - lam * (acc_ref[1, :, :hw] / acc_ref[1, :, hw:])
    o = o * lax.rsqrt(jnp.mean(o * o, axis=-1, keepdims=True) + DA_SUBLN_EPS) * sw_ref[...]
    o_ref[0] = o * (1.0 - lam_init)


def _attn(z, lq1, lk1, lq2, lk2, subln_w, lam_init):
    b_, t_, _ = z.shape
    tq = min(ATT_TILE, t_)
    hw = 2 * DA_HD
    q0, k0, v0 = RW_COLS // hw, (RW_COLS + DA_W) // hw, (RW_COLS + 2 * DA_W) // hw
    small = lambda n: pl.BlockSpec((1, n), lambda b, h, qi: (0, 0))
    return pl.pallas_call(
        functools.partial(_attn_kernel, lam_init=lam_init),
        out_shape=jax.ShapeDtypeStruct((b_, t_, DA_W), F32),
        grid=(b_, DA_HEADS, t_ // tq),
        in_specs=[
            pl.BlockSpec((1, tq, hw), lambda b, h, qi: (b, qi, q0 + h)),
            pl.BlockSpec((1, t_, hw), lambda b, h, qi: (b, 0, k0 + h)),
            pl.BlockSpec((1, t_, hw), lambda b, h, qi: (b, 0, v0 + h)),
            small(DA_HD), small(DA_HD), small(DA_HD), small(DA_HD), small(hw),
        ],
        out_specs=pl.BlockSpec((1, tq, hw), lambda b, h, qi: (b, qi, h)),
        scratch_shapes=[pltpu.VMEM((2, tq, hw), BF16), pltpu.VMEM((2, tq, hw), F32),
                        pltpu.VMEM((2, tq, 2 * hw), F32)],
        compiler_params=_params(("parallel", "parallel", "arbitrary")),
        name="diff_attn",
    )(z, z, z, lq1, lk1, lq2, lk2, subln_w)


def _gelu_tanh(x):
    return 0.5 * x * (1.0 + jnp.tanh(math.sqrt(2.0 / math.pi) * (x + 0.044715 * (x * x * x))))


def _s5_kernel(u_ref, bmat_ref, pw_ref, step_ref, cmat_ref, d_ref, wglu_ref, bglu_ref, o_ref,
               xs_ref, carry_ref):
    @pl.when(pl.program_id(1) == 0)
    def _():
        carry_ref[...] = jnp.zeros_like(carry_ref)

    u = u_ref[0]
    tt = u.shape[0]
    n = SSM_N
    hw, hn = SSM_W // 2, n // 2
    ub = u.astype(BF16)
    for h in range(2):
        ch = slice(h * hw, (h + 1) * hw)
        for part in (0, n):
            st = slice(part + h * hn, part + (h + 1) * hn)
            xs_ref[:, st] = _dot(ub[:, ch], bmat_ref[ch, st])
    pr, pi = pw_ref[0], pw_ref[1]

    def block(i, carry):
        cr, ci = carry
        rows = pl.ds(pl.multiple_of(i * SUBLANES, SUBLANES), SUBLANES)
        xr = xs_ref[rows, :n]
        xi = xs_ref[rows, n:]
        for lvl, d in enumerate((1, 2, 4)):
            ar, ai = step_ref[lvl, :, :n], step_ref[lvl, :, n:]
            sr, si = pltpu.roll(xr, d, 0), pltpu.roll(xi, d, 0)
            xr, xi = xr + ar * sr - ai * si, xi + ar * si + ai * sr
        xr, xi = xr + pr * cr - pi * ci, xi + pr * ci + pi * cr
        xs_ref[rows, :n] = xr
        xs_ref[rows, n:] = xi
        return xr[SUBLANES - 1:, :], xi[SUBLANES - 1:, :]

    cr, ci = lax.fori_loop(0, tt // SUBLANES, block, (carry_ref[0:1, :], carry_ref[1:2, :]))
    carry_ref[0:1, :] = cr
    carry_ref[1:2, :] = ci
    ys = []
    for h in range(2):
        ch = slice(h * hw, (h + 1) * hw)
        re, im = slice(h * hn, (h + 1) * hn), slice(n + h * hn, n + (h + 1) * hn)
        ys.append(_dot(xs_ref[:, re].astype(BF16), cmat_ref[re, ch]) + _dot(xs_ref[:, im].astype(BF16), cmat_ref[im, ch]))
    y = jnp.concatenate(ys, axis=1) + d_ref[...] * u
    y = _gelu_tanh(y)
    o_ref[0] = y * jax.nn.sigmoid(_dot(y.astype(BF16), wglu_ref[...]) + bglu_ref[...])


def _s5(z, bmat, pw, step, cmat, d_skip, w_glu, b_glu):
    b_, t_, _ = z.shape
    tt = min(SSM_TSTEP, t_)
    n = SSM_N
    full = lambda shape: pl.BlockSpec(shape, lambda b, i: (0,) * len(shape))
    return pl.pallas_call(
        _s5_kernel,
        out_shape=jax.ShapeDtypeStruct((b_, t_, SSM_W), F32),
        grid=(b_, t_ // tt),
        in_specs=[
            pl.BlockSpec((pl.Element(1), pl.Element(tt), pl.Element(SSM_W)), lambda b, i: (b, i * tt, SSM_COL0)),
            full((SSM_W, 2 * n)), full((2, SUBLANES, n)), full((3, SUBLANES, 2 * n)), full((2 * n, SSM_W)),
            full((1, SSM_W)), full((SSM_W, SSM_W)), full((1, SSM_W)),
        ],
        out_specs=pl.BlockSpec((1, tt, SSM_W), lambda b, i: (b, i, 0)),
        scratch_shapes=[pltpu.VMEM((tt, 2 * n), F32), pltpu.VMEM((2, n), F32)],
        compiler_params=_params(("parallel", "arbitrary")),
        name="s5",
    )(z, bmat, pw, step, cmat, d_skip, w_glu, b_glu)


def _s5_discretise(a_re, a_im, log_dt, b_re, b_im, c_re, c_im):
    g_, n_, c_ = SSM_GROUPS, SSM_STATE, SSM_GROUP
    dt = jnp.exp(log_dt)[:, None]
    mag = jnp.exp(dt * a_re)
    abar_r, abar_i = mag * jnp.cos(dt * a_im), mag * jnp.sin(dt * a_im)
    den = a_re * a_re + a_im * a_im
    nr, ni = abar_r - 1.0, abar_i
    coef_r, coef_i = (nr * a_re + ni * a_im) / den, (ni * a_re - nr * a_im) / den
    bbar_r = coef_r[..., None] * b_re - coef_i[..., None] * b_im
    bbar_i = coef_r[..., None] * b_im + coef_i[..., None] * b_re
    eye = jnp.eye(g_, dtype=F32)
    bd_in = lambda m: jnp.einsum('gnc,gh->gchn', m, eye).reshape(g_ * c_, g_ * n_)
    bmat = jnp.concatenate([bd_in(bbar_r), bd_in(bbar_i)], axis=1).astype(BF16)
    bd_out = lambda m: jnp.einsum('gcn,gh->gnhc', m, eye).reshape(g_ * n_, g_ * c_)
    cmat = jnp.concatenate([bd_out(c_re), -bd_out(c_im)], axis=0).astype(BF16)
    ar, ai = abar_r.reshape(1, -1), abar_i.reshape(1, -1)
    pows_r, pows_i = [ar], [ai]
    for _ in range(SUBLANES - 1):
        pr, pi = pows_r[-1], pows_i[-1]
        pows_r.append(pr * ar - pi * ai)
        pows_i.append(pr * ai + pi * ar)
    pw = jnp.stack([jnp.concatenate(pows_r, axis=0), jnp.concatenate(pows_i, axis=0)])
    t_idx = jnp.arange(SUBLANES)[:, None]
    step = jnp.stack([jnp.where(t_idx >= d, jnp.concatenate([pows_r[d - 1], pows_i[d - 1]], axis=1), 0.0)
                      for d in (1, 2, 4)])
    return bmat, pw, step, cmat


def _outproj_kernel(x_ref, yrw_ref, yda_ref, yss_ref, w1_ref, w2_ref, w3_ref, g_ref, o_ref):
    y = (_dot(yrw_ref[...].astype(BF16), w1_ref[...]) + _dot(yda_ref[...].astype(BF16), w2_ref[...])
         + _dot(yss_ref[...].astype(BF16), w3_ref[...]))
    o_ref[...] = x_ref[...] + _rms(y, g_ref[...], NORM_EPS)


def _outproj(x, y_rw, y_da, y_ss, w, g):
    m, d = x.shape
    tm = min(ROW_TILE, m)
    rowblk = lambda n: pl.BlockSpec((tm, n), lambda i: (i, 0))
    assert RW_W == DA_W and (RW_W + DA_W) % SSM_W == 0
    band = lambda rows, idx: pl.BlockSpec((rows, d), lambda i: (idx, 0))
    return pl.pallas_call(
        _outproj_kernel,
        out_shape=jax.ShapeDtypeStruct((m, d), F32),
        grid=(m // tm,),
        in_specs=[rowblk(d), rowblk(RW_W), rowblk(DA_W), rowblk(SSM_W),
                  band(RW_W, 0), band(DA_W, 1), band(SSM_W, (RW_W + DA_W) // SSM_W),
                  pl.BlockSpec((1, d), lambda i: (0, 0))],
        out_specs=rowblk(d),
        compiler_params=_params(("parallel",)),
        name="outproj",
    )(x, y_rw, y_da, y_ss, w, w, w, g)


def kernel(x, ffn1_pre_g, ffn1_w_gu, ffn1_w_down, ffn1_post_g, mix_pre_g, w_in, rw_mu, rw_w0, rw_w2, rw_a0, rw_a2, rw_g2, rw_k_k, rw_k_a, rw_r_k, rw_gn_w, rw_gn_b, da_lq1, da_lk1, da_lq2, da_lk2, da_subln_w, ssm_a_re, ssm_a_im, ssm_log_dt, ssm_b_re, ssm_b_im, ssm_c_re, ssm_c_im, ssm_d, ssm_w_glu, ssm_b_glu, w_out, mix_post_g, ffn2_pre_g, ffn2_w_gu, ffn2_w_down, ffn2_post_g):
    b_, t_, d_ = x.shape
    m = b_ * t_
    row = lambda a: a.reshape(1, -1)
    xf = x.reshape(m, d_)
    for l in range(DEPTH):
        xf, xn = _ffn(xf, row(ffn1_pre_g[l]), ffn1_w_gu, ffn1_w_down, row(ffn1_post_g[l]), l, row(mix_pre_g[l]))

        z = _inproj(xn, w_in, l).reshape(b_, t_, IN_COLS)

        feats = _rwprep(z, row(rw_mu[l]), row(rw_w0[l]), rw_w2[l], row(rw_a0[l]), rw_a2[l], rw_g2[l],
                        row(rw_k_k[l]), row(rw_k_a[l]))
        per_head = lambda a: a.reshape(RW_HEADS, 1, RW_HD)
        r, lw, k, v, kk, b, g = feats
        maps = _rwchunk(r, lw, k, v, kk, b, per_head(rw_r_k[l]))
        y_rw = _rwstate(*maps, g, row(rw_gn_w[l]), row(rw_gn_b[l]))

        lam_init = 0.8 - 0.6 * math.exp(-0.3 * l)
        y_da = _attn(z, row(da_lq1[l]), row(da_lk1[l]), row(da_lq2[l]), row(da_lk2[l]),
                     row(da_subln_w[l]), lam_init)

        bmat, pw, step, cmat = _s5_discretise(ssm_a_re[l], ssm_a_im[l], ssm_log_dt[l], ssm_b_re[l],
                                              ssm_b_im[l], ssm_c_re[l], ssm_c_im[l])
        y_ss = _s5(z, bmat, pw, step, cmat, row(ssm_d[l]), ssm_w_glu[l].astype(BF16), row(ssm_b_glu[l]))

        xf = _outproj(xf, y_rw.reshape(m, RW_W), y_da.reshape(m, DA_W), y_ss.reshape(m, SSM_W),
                      w_out[l].astype(BF16), row(mix_post_g[l]))

        xf, = _ffn(xf, row(ffn2_pre_g[l]), ffn2_w_gu, ffn2_w_down, row(ffn2_post_g[l]), l)
    return xf.reshape(b_, t_, d_)
```

```python
import functools
import math

import jax
import jax.numpy as jnp
from jax import lax
from jax.experimental import pallas as pl
from jax.experimental.pallas import tpu as pltpu

F32 = jnp.float32
BF16 = jnp.bfloat16
HIGHEST = lax.Precision.HIGHEST

D_MODEL = 2048
DEPTH = 2
RW_HEADS = 12
RW_HD = 64
RW_W = RW_HEADS * RW_HD
RW_DECAY_LORA = 64
RW_AAA_LORA = 64
RW_GATE_LORA = 128
RW_GN_EPS = 64e-5
DA_HEADS = 6
DA_HD = 64
DA_W = DA_HEADS * 2 * DA_HD
DA_SUBLN_EPS = 1e-5
SSM_W = D_MODEL - RW_W - DA_W
SSM_GROUP = 16
SSM_GROUPS = SSM_W // SSM_GROUP
SSM_STATE = 64
SSM_N = SSM_GROUPS * SSM_STATE
RW_COLS = 3 * RW_W + RW_DECAY_LORA + RW_AAA_LORA + RW_GATE_LORA
DA_COLS = 3 * DA_W
D_FF = 5504
NORM_EPS = 1e-6

LANES = 128
SUBLANES = 8
VMEM_LIMIT = 56 * 1024 * 1024
FF_TILE = 256
FFN_ROW_TILE = 1024
IN_ROW_TILE = 1024
IN_COLS = RW_COLS + DA_COLS + SSM_W
IN_TILE = IN_COLS // 3
SSM_COL0 = RW_COLS + DA_COLS
ROW_TILE = 512
RW_CHUNK = 64
RW_TSTEP = 512
RW_STATE_HEADS = 4
RW_MAP_HEADS = 4
RW_MAP_TSTEP = 512
ATT_TILE = 512
SSM_TSTEP = 512
NEG_BIG = -1e30


def _dot(a, b, precision=None):
    return jnp.dot(a, b, preferred_element_type=F32, precision=precision)


def _dot_nt(a, b, precision=None):
    return lax.dot_general(a, b, (((1,), (1,)), ((), ())),
                           preferred_element_type=F32, precision=precision)


def _rms(x, g, eps):
    return x * lax.rsqrt(jnp.mean(x * x, axis=-1, keepdims=True) + eps) * g


def _params(sem):
    return pltpu.CompilerParams(dimension_semantics=sem, vmem_limit_bytes=VMEM_LIMIT)


def _ff_offset(j, base=0):
    return (base // LANES + jnp.minimum(j * (FF_TILE // LANES), (D_FF - FF_TILE) // LANES)) * LANES


def _ffn_kernel(*refs, emit_next):
    if emit_next:
        x_ref, pre_ref, wg_ref, wu_ref, wd_ref, post_ref, ng_ref, o_ref, nx_ref, xn_ref = refs
    else:
        x_ref, pre_ref, wg_ref, wu_ref, wd_ref, post_ref, o_ref, xn_ref = refs
    j = pl.program_id(1)

    @pl.when(j == 0)
    def _():
        xn_ref[...] = _rms(x_ref[...], pre_ref[...], NORM_EPS).astype(BF16)
        o_ref[...] = jnp.zeros_like(o_ref)

    xn = xn_ref[...]
    gate = _dot(xn, wg_ref[0].astype(BF16))
    up = _dot(xn, wu_ref[0].astype(BF16))
    h = gate * jax.nn.sigmoid(gate) * up
    unit = _ff_offset(j) + lax.broadcasted_iota(jnp.int32, h.shape, 1)
    h = jnp.where(unit >= j * FF_TILE, h, 0.0).astype(BF16)
    o_ref[...] += _dot(h, wd_ref[0].astype(BF16))

    @pl.when(j == pl.num_programs(1) - 1)
    def _():
        y = x_ref[...] + 0.5 * _rms(o_ref[...], post_ref[...], NORM_EPS)
        o_ref[...] = y
        if emit_next:
            nx_ref[...] = _rms(y, ng_ref[...], NORM_EPS).astype(BF16)


def _ffn(x, pre_g, w_gu, w_down, post_g, layer, next_g=None):
    m, d = x.shape
    tm, tf = min(FFN_ROW_TILE, m), FF_TILE
    once = pl.Buffered(1)
    emit_next = next_g is not None
    vec = pl.BlockSpec((1, d), lambda i, j: (0, 0))
    row_out = pl.BlockSpec((tm, d), lambda i, j: (i, 0), pipeline_mode=once)
    return pl.pallas_call(
        functools.partial(_ffn_kernel, emit_next=emit_next),
        out_shape=[jax.ShapeDtypeStruct((m, d), F32)] + [jax.ShapeDtypeStruct((m, d), BF16)] * emit_next,
        grid=(m // tm, pl.cdiv(D_FF, tf)),
        in_specs=[
            pl.BlockSpec((tm, d), lambda i, j: (i, 0)),
            vec,
            pl.BlockSpec((pl.Element(1), pl.Element(d), pl.Element(tf)), lambda i, j: (layer, 0, _ff_offset(j))),
            pl.BlockSpec((pl.Element(1), pl.Element(d), pl.Element(tf)), lambda i, j: (layer, 0, _ff_offset(j, D_FF))),
            pl.BlockSpec((pl.Element(1), pl.Element(tf), pl.Element(d)), lambda i, j: (layer, _ff_offset(j), 0)),
            vec,
        ] + [vec] * emit_next,
        out_specs=[row_out] * (1 + emit_next),
        scratch_shapes=[pltpu.VMEM((tm, d), BF16)],
        compiler_params=_params(("parallel", "arbitrary")),
        name="ffn",
    )(x, pre_g, w_gu, w_gu, w_down, post_g, *([next_g] * emit_next))


def _inproj_kernel(xn_ref, w_ref, o_ref, wb_ref):
    @pl.when(pl.program_id(1) == 0)
    def _():
        wb_ref[...] = w_ref[0].astype(BF16)

    o_ref[...] = _dot(xn_ref[...], wb_ref[...])


def _inproj(xn, w_in, layer):
    m, d = xn.shape
    tm, tn = min(IN_ROW_TILE, m), IN_TILE
    return pl.pallas_call(
        _inproj_kernel,
        out_shape=jax.ShapeDtypeStruct((m, IN_COLS), F32),
        grid=(IN_COLS // tn, m // tm),
        in_specs=[
            pl.BlockSpec((tm, d), lambda j, i: (i, 0)),
            pl.BlockSpec((pl.Element(1), pl.Element(d), pl.Element(tn)), lambda j, i: (layer, 0, j * tn),
                         pipeline_mode=pl.Buffered(1)),
        ],
        out_specs=pl.BlockSpec((tm, tn), lambda j, i: (i, j)),
        scratch_shapes=[pltpu.VMEM((d, tn), BF16)],
        compiler_params=_params(("parallel", "arbitrary")),
        name="inproj",
    )(xn, w_in)


def _softplus(x):
    return jnp.maximum(x, 0.0) + jnp.log(1.0 + jnp.exp(-jnp.abs(x)))


def _head_sums(x, first):
    s_first = jnp.sum(jnp.where(first, x, 0.0), axis=-1, keepdims=True)
    s_all = jnp.sum(x, axis=-1, keepdims=True)
    return jnp.where(first, s_first, s_all - s_first)


_NN = ((1,), (0,))
_NT = ((1,), (1,))
_TN = ((0,), (0,))


def _bdot(a, b, dims=_NN):
    return lax.dot_general(a.astype(BF16), b.astype(BF16), (dims, ((), ())), preferred_element_type=F32)


def _unit_lower_inverses(ns, row, col):
    c = ns[0].shape[0]
    eye = (row == col).astype(F32)
    base = SUBLANES
    diag = (row // base) == (col // base)
    a0 = [jnp.where(diag, n, 0.0) for n in ns]
    a2 = [_bdot(a, a) for a in a0]
    a4 = [_bdot(a, a) for a in a2]
    xs = [_bdot(eye + p, eye + q) for p, q in zip(a0, a2)]
    xs = [_bdot(p, eye + q) for p, q in zip(xs, a4)]
    m = base
    while m < c:
        off = ((row // (2 * m)) == (col // (2 * m))) & ((row // m) != (col // m))
        xe = [_bdot(p, jnp.where(off, n, 0.0)) for p, n in zip(xs, ns)]
        xs = [p + _bdot(q, p) for p, q in zip(xs, xe)]
        m *= 2
    return xs


def _rwchunk_kernel(zr_ref, zk_ref, zv_ref, zl_ref, pr_ref, pk_ref, pv_ref, pl_ref,
                    mur_ref, muk_ref, muv_ref, mul_ref, w0_ref, w2_ref, a0_ref, a2_ref, g2_ref, kk_ref, ka_ref, rk_ref,
                    rp_o, y0_o, tm_o, sadd_o, bonus_o, g_o):
    c, kd = RW_CHUNK, RW_HD
    tstep, gw = zr_ref.shape[1:]
    nheads, nchunks = gw // kd, tstep // c
    first_row = lax.broadcasted_iota(jnp.int32, (tstep, gw), 0) == 0
    at_start = pl.program_id(2) == 0

    def token_shift(z_ref, prev_ref, mu_ref):
        z = z_ref[0]
        before = jnp.where(at_start, 0.0, prev_ref[0, SUBLANES - 1:SUBLANES, :])
        z_prev = jnp.where(first_row, before, pltpu.roll(z, 1, 0))
        return z + (z_prev - z) * mu_ref[...]

    r = token_shift(zr_ref, pr_ref, mur_ref)
    k = token_shift(zk_ref, pk_ref, muk_ref)
    v = token_shift(zv_ref, pv_ref, muv_ref)
    lora = token_shift(zl_ref, pl_ref, mul_ref)
    o1, o2 = RW_DECAY_LORA, RW_DECAY_LORA + RW_AAA_LORA
    w = -_softplus(-(w0_ref[...] + _dot(jnp.tanh(lora[:, :o1]), w2_ref[...], HIGHEST))) - 0.5
    lw = -jnp.exp(w)
    a = jax.nn.sigmoid(a0_ref[...] + _dot(lora[:, o1:o2], a2_ref[...], HIGHEST))
    g_o[0] = _dot(jax.nn.sigmoid(lora[:, o2:]), g2_ref[...], HIGHEST)
    kk = k * kk_ref[...]
    k = k * (1.0 + (a - 1.0) * ka_ref[...])
    first = lax.broadcasted_iota(jnp.int32, (tstep, LANES), 1) < kd
    blocks = [slice(j * LANES, (j + 1) * LANES) for j in range(gw // LANES)]
    norm = jnp.concatenate([jnp.sqrt(_head_sums(kk[:, s] * kk[:, s], first)) for s in blocks], axis=1)
    kk = kk / jnp.maximum(norm, 1e-12)
    b = kk * a
    rkr = r * k * rk_ref[...]
    bonus_o[0] = jnp.concatenate([_head_sums(rkr[:, s], first) for s in blocks], axis=1) * v

    row = lax.broadcasted_iota(jnp.int32, (c, c), 0)
    col = lax.broadcasted_iota(jnp.int32, (c, c), 1)
    strict = row > col
    incl = row >= col
    tri = incl.astype(BF16)
    chunk_rows = [slice(ci * c, (ci + 1) * c) for ci in range(nchunks)]
    lw_c = [lw[rows] for rows in chunk_rows]
    lw_hi = [x.astype(BF16) for x in lw_c]
    lc_hi = [_bdot(tri, h) for h in lw_hi]
    lc_lo = [_bdot(tri, x - h.astype(F32)) for x, h in zip(lw_c, lw_hi)]
    lc = [p + q for p, q in zip(lc_hi, lc_lo)]
    e_inc = [jnp.exp(x) for x in lc]
    e_inv = [jnp.exp(-x) for x in lc]
    e_end = [jnp.exp(x[c - 1:c, :] - x) for x in lc]
    wide = {
        "at": [-kk[rows] * jnp.exp(x - w_) for rows, x, w_ in zip(chunk_rows, lc, lw_c)],
        "rt": [r[rows] * e for rows, e in zip(chunk_rows, e_inc)],
        "bt": [b[rows] * e for rows, e in zip(chunk_rows, e_inv)],
        "kt": [k[rows] * e for rows, e in zip(chunk_rows, e_inv)],
        "be": [b[rows] * e for rows, e in zip(chunk_rows, e_end)],
        "ke": [k[rows] * e for rows, e in zip(chunk_rows, e_end)],
        "v": [v[rows] for rows in chunk_rows],
        "w_end": [e[c - 1:c, :] for e in e_inc],
    }
    items = [(hh, ci) for hh in range(nheads) for ci in range(nchunks)]
    per_item = lambda name: [wide[name][ci][:, hh * kd:(hh + 1) * kd] for hh, ci in items]
    at, rt, bt, kt, be, ke, vi, w_end = (per_item(n) for n in ("at", "rt", "bt", "kt", "be", "ke", "v", "w_end"))
    m1 = [_bdot(jnp.concatenate([p, q], axis=0), jnp.concatenate([bb, kx], axis=0), _NT)
          for p, q, bb, kx in zip(at, rt, bt, kt)]
    a_ab = [jnp.where(strict, m[:c, :c], 0.0) for m in m1]
    a_ak = [jnp.where(strict, m[:c, c:], 0.0) for m in m1]
    r_b = [jnp.where(incl, m[c:, :c], 0.0) for m in m1]
    r_k = [jnp.where(incl, m[c:, c:], 0.0) for m in m1]
    av = [_bdot(p, q) for p, q in zip(a_ak, vi)]
    rkv = [_bdot(p, q) for p, q in zip(r_k, vi)]
    kev = [_bdot(p, q, _TN) for p, q in zip(ke, vi)]
    xs = _unit_lower_inverses(a_ab, row, col)
    z = [_bdot(x, jnp.concatenate([p, q], axis=1)) for x, p, q in zip(xs, at, av)]
    rw = [_bdot(p, q) for p, q in zip(r_b, z)]
    tz = [_bdot(p, q, _TN) for p, q in zip(be, z)]
    eye_k = lax.broadcasted_iota(jnp.int32, (kd, kd), 0) == lax.broadcasted_iota(jnp.int32, (kd, kd), 1)
    for ci in range(nchunks):
        rows = pl.ds(ci * c, c)
        ids = [hh * nchunks + ci for hh in range(nheads)]
        side_by_side = lambda pieces: jnp.concatenate(pieces, axis=1)
        rp_o[0, rows, :] = side_by_side([rt[i] + rw[i][:, :kd] for i in ids])
        y0_o[0, rows, :] = side_by_side([rw[i][:, kd:] + rkv[i] for i in ids])
        tm_o[0, rows, :] = side_by_side(
            [jnp.where(eye_k, jnp.broadcast_to(w_end[i], (kd, kd)), 0.0) + tz[i][:, :kd] for i in ids])
        sadd_o[0, rows, :] = side_by_side([tz[i][:, kd:] + kev[i] for i in ids])


def _rwchunk(z, mu, w0, w2, a0, a2, g2, k_k, k_a, r_k):
    b_, t_, _ = z.shape
    gw = RW_MAP_HEADS * RW_HD
    assert RW_CHUNK == RW_HD
    assert RW_DECAY_LORA + RW_AAA_LORA + RW_GATE_LORA == gw and RW_W % gw == 0
    tstep = min(RW_MAP_TSTEP, t_)
    nprev = tstep // SUBLANES
    col_r, col_k, col_v, col_l = 0, RW_W // gw, 2 * RW_W // gw, 3 * RW_W // gw
    seq = lambda c0, own: pl.BlockSpec((1, tstep, gw), lambda bb, h, i: (bb, i, c0 + own * h))
    prev = lambda c0, own: pl.BlockSpec(
        (1, SUBLANES, gw), lambda bb, h, i: (bb, jnp.maximum(i * nprev - 1, 0), c0 + own * h))
    vec = lambda c0, own: pl.BlockSpec((1, gw), lambda bb, h, i: (0, c0 + own * h))
    mat = lambda rows: pl.BlockSpec((rows, gw), lambda bb, h, i: (0, h))
    places = [(col_r, 1), (col_k, 1), (col_v, 1), (col_l, 0)]
    out = pl.BlockSpec((1, tstep, gw), lambda bb, h, i: (bb, i, h))
    return pl.pallas_call(
        _rwchunk_kernel,
        out_shape=[jax.ShapeDtypeStruct((b_, t_, RW_W), F32)] * 6,
        grid=(b_, RW_W // gw, t_ // tstep),
        in_specs=([seq(*p) for p in places] + [prev(*p) for p in places] + [vec(*p) for p in places]
                  + [vec(0, 1), mat(RW_DECAY_LORA), vec(0, 1), mat(RW_AAA_LORA), mat(RW_GATE_LORA),
                     vec(0, 1), vec(0, 1), vec(0, 1)]),
        out_specs=[out] * 6,
        compiler_params=_params(("parallel", "parallel", "parallel")),
        name="rwkv_chunk",
    )(z, z, z, z, z, z, z, z, mu, mu, mu, mu, w0, w2, a0, a2, g2, k_k, k_a, r_k)


def _rwstate_kernel(rp_ref, y0_ref, tm_ref, sadd_ref, bonus_ref, g_ref, gnw_ref, gnb_ref, o_ref, s_ref, y_ref):
    @pl.when(pl.program_id(1) == 0)
    def _():
        s_ref[...] = jnp.zeros_like(s_ref)

    c, kd = RW_CHUNK, RW_HD
    tstep, width = y_ref.shape
    gw = RW_STATE_HEADS * kd
    own_block = (lax.broadcasted_iota(jnp.int32, (gw, gw), 0) // kd) == (lax.broadcasted_iota(jnp.int32, (gw, gw), 1) // kd)
    for ci in range(tstep // c):
        rows = pl.ds(ci * c, c)
        for gi in range(width // gw):
            lanes = slice(gi * gw, (gi + 1) * gw)
            st = s_ref[:, lanes].astype(BF16)
            st_bd = jnp.where(own_block, jnp.concatenate([st] * RW_STATE_HEADS, axis=0), 0.0)
            lhs = jnp.concatenate([tm_ref[0, rows, lanes], rp_ref[0, rows, lanes]], axis=0).astype(BF16)
            res = _dot(lhs, st_bd)
            s_ref[:, lanes] = res[:c] + sadd_ref[0, rows, lanes]
            y_ref[rows, lanes] = res[c:] + y0_ref[0, rows, lanes]

    first = lax.broadcasted_iota(jnp.int32, (tstep, LANES), 1) < kd
    head_mean = lambda x: _head_sums(x, first) * (1.0 / kd)
    for j in range(width // LANES):
        lanes = slice(j * LANES, (j + 1) * LANES)
        y = y_ref[:, lanes]
        d = y - head_mean(y)
        yn = d * lax.rsqrt(head_mean(d * d) + RW_GN_EPS) * gnw_ref[:, lanes] + gnb_ref[:, lanes]
        o_ref[0, :, lanes] = (yn + bonus_ref[0, :, lanes]) * g_ref[0, :, lanes]


def _rwstate(rp, y0, tm, sadd, bonus, g, gn_w, gn_b):
    b_, t_, w_ = rp.shape
    tstep = min(RW_TSTEP, t_)
    seq = pl.BlockSpec((1, tstep, w_), lambda bb, i: (bb, i, 0))
    par = pl.BlockSpec((1, w_), lambda bb, i: (0, 0))
    return pl.pallas_call(
        _rwstate_kernel,
        out_shape=jax.ShapeDtypeStruct((b_, t_, w_), F32),
        grid=(b_, t_ // tstep),
        in_specs=[seq] * 6 + [par] * 2,
        out_specs=seq,
        scratch_shapes=[pltpu.VMEM((RW_HD, w_), F32), pltpu.VMEM((tstep, w_), F32)],
        compiler_params=_params(("parallel", "arbitrary")),
        name="rwkv_state",
    )(rp, y0, tm, sadd, bonus, g, gn_w, gn_b)


def _attn_kernel(q_ref, k_ref, v_ref, lq1_ref, lk1_ref, lq2_ref, lk2_ref, sw_ref, o_ref,
                 qs_ref, m_ref, acc_ref, *, lam_init):
    qi = pl.program_id(2)
    t = q_ref.shape[1]
    q = q_ref[0] * (DA_HD ** -0.5)
    lane = lax.broadcasted_iota(jnp.int32, q.shape, 1)
    qs_ref[0] = jnp.where(lane < DA_HD, q, 0.0).astype(BF16)
    qs_ref[1] = jnp.where(lane >= DA_HD, q, 0.0).astype(BF16)
    m_ref[...] = jnp.full_like(m_ref, NEG_BIG)
    acc_ref[...] = jnp.zeros_like(acc_ref)
    hw = q.shape[1]
    ones = jnp.ones((t, hw), BF16)
    subheads = range(2)

    def key_tiles(js, diagonal):
        kbs, vbs = [], []
        for j in js:
            rows = pl.ds(pl.multiple_of(j * t, t), t)
            kbs.append(k_ref[0, rows, :].astype(BF16))
            vbs.append(jnp.concatenate([v_ref[0, rows, :].astype(BF16), ones], axis=1))
        s = [[_dot_nt(qs_ref[c], kb) for kb in kbs] for c in subheads]
        if diagonal:
            causal = (lax.broadcasted_iota(jnp.int32, (t, t), 1) <= lax.broadcasted_iota(jnp.int32, (t, t), 0))
            for c in subheads:
                s[c][-1] = jnp.where(causal, s[c][-1], NEG_BIG)
        m_old = [m_ref[c] for c in subheads]
        m_new = []
        for c in subheads:
            m = m_old[c]
            for x in s[c]:
                m = jnp.maximum(m, jnp.max(x, axis=-1, keepdims=True))
            m_new.append(m)
        m_wide = [jnp.concatenate([m] * (t // hw), axis=1) for m in m_new]
        p = [[jnp.exp(x - m_wide[c]).astype(BF16) for x in s[c]] for c in subheads]
        pv = [sum(_dot(x, vb) for x, vb in zip(p[c], vbs)) for c in subheads]
        for c in subheads:
            alpha = jnp.exp(m_old[c] - m_new[c])
            acc_ref[c] = jnp.concatenate([alpha, alpha], axis=1) * acc_ref[c] + pv[c]
            m_ref[c] = m_new[c]

    def pair_below_diagonal(jj, carry):
        key_tiles([2 * jj, 2 * jj + 1], False)
        return carry

    lax.fori_loop(0, qi // 2, pair_below_diagonal, 0)

    @pl.when(qi % 2 == 1)
    def _():
        key_tiles([qi - 1, qi], True)

    @pl.when(qi % 2 == 0)
    def _():
        key_tiles([qi], True)

    lam = (jnp.exp(jnp.sum(lq1_ref[...] * lk1_ref[...], axis=-1, keepdims=True))
           - jnp.exp(jnp.sum(lq2_ref[...] * lk2_ref[...], axis=-1, keepdims=True)) + lam_init)
    o = acc_ref[0, :, :hw] / acc_ref[0, :, hw:] - lam * (acc_ref[1, :, :hw] / acc_ref[1, :, hw:])
    o = o * lax.rsqrt(jnp.mean(o * o, axis=-1, keepdims=True) + DA_SUBLN_EPS) * sw_ref[...]
    o_ref[0] = o * (1.0 - lam_init)


def _attn(z, lq1, lk1, lq2, lk2, subln_w, lam_init):
    b_, t_, _ = z.shape
    tq = min(ATT_TILE, t_)
    hw = 2 * DA_HD
    q0, k0, v0 = RW_COLS // hw, (RW_COLS + DA_W) // hw, (RW_COLS + 2 * DA_W) // hw
    small = lambda n: pl.BlockSpec((1, n), lambda b, h, qi: (0, 0))
    return pl.pallas_call(
        functools.partial(_attn_kernel, lam_init=lam_init),
        out_shape=jax.ShapeDtypeStruct((b_, t_, DA_W), F32),
        grid=(b_, DA_HEADS, t_ // tq),
        in_specs=[
            pl.BlockSpec((1, tq, hw), lambda b, h, qi: (b, qi, q0 + h)),
            pl.BlockSpec((1, t_, hw), lambda b, h, qi: (b, 0, k0 + h)),
            pl.BlockSpec((1, t_, hw), lambda b, h, qi: (b, 0, v0 + h)),
            small(DA_HD), small(DA_HD), small(DA_HD), small(DA_HD), small(hw),
        ],
        out_specs=pl.BlockSpec((1, tq, hw), lambda b, h, qi: (b, qi, h)),
        scratch_shapes=[pltpu.VMEM((2, tq, hw), BF16), pltpu.VMEM((2, tq, hw), F32),
                        pltpu.VMEM((2, tq, 2 * hw), F32)],
        compiler_params=_params(("parallel", "parallel", "arbitrary")),
        name="diff_attn",
    )(z, z, z, lq1, lk1, lq2, lk2, subln_w)


def _gelu_tanh(x):
    return 0.5 * x * (1.0 + jnp.tanh(math.sqrt(2.0 / math.pi) * (x + 0.044715 * (x * x * x))))


def _s5_kernel(u_ref, bmat_ref, pw_ref, step_ref, cmat_ref, d_ref, wglu_ref, bglu_ref, o_ref,
               xs_ref, carry_ref):
    @pl.when(pl.program_id(1) == 0)
    def _():
        carry_ref[...] = jnp.zeros_like(carry_ref)

    u = u_ref[0]
    tt = u.shape[0]
    n = SSM_N
    hw, hn = SSM_W // 2, n // 2
    ub = u.astype(BF16)
    for h in range(2):
        ch = slice(h * hw, (h + 1) * hw)
        for part in (0, n):
            st = slice(part + h * hn, part + (h + 1) * hn)
            xs_ref[:, st] = _dot(ub[:, ch], bmat_ref[ch, st])
    pr, pi = pw_ref[0], pw_ref[1]

    def block(i, carry):
        cr, ci = carry
        rows = pl.ds(pl.multiple_of(i * SUBLANES, SUBLANES), SUBLANES)
        xr = xs_ref[rows, :n]
        xi = xs_ref[rows, n:]
        for lvl, d in enumerate((1, 2, 4)):
            ar, ai = step_ref[lvl, :, :n], step_ref[lvl, :, n:]
            sr, si = pltpu.roll(xr, d, 0), pltpu.roll(xi, d, 0)
            xr, xi = xr + ar * sr - ai * si, xi + ar * si + ai * sr
        xr, xi = xr + pr * cr - pi * ci, xi + pr * ci + pi * cr
        xs_ref[rows, :n] = xr
        xs_ref[rows, n:] = xi
        return xr[SUBLANES - 1:, :], xi[SUBLANES - 1:, :]

    cr, ci = lax.fori_loop(0, tt // SUBLANES, block, (carry_ref[0:1, :], carry_ref[1:2, :]))
    carry_ref[0:1, :] = cr
    carry_ref[1:2, :] = ci
    ys = []
    for h in range(2):
        ch = slice(h * hw, (h + 1) * hw)
        re, im = slice(h * hn, (h + 1) * hn), slice(n + h * hn, n + (h + 1) * hn)
        ys.append(_dot(xs_ref[:, re].astype(BF16), cmat_ref[re, ch]) + _dot(xs_ref[:, im].astype(BF16), cmat_ref[im, ch]))
    y = jnp.concatenate(ys, axis=1) + d_ref[...] * u
    y = _gelu_tanh(y)
    o_ref[0] = y * jax.nn.sigmoid(_dot(y.astype(BF16), wglu_ref[...]) + bglu_ref[...])


def _s5(z, bmat, pw, step, cmat, d_skip, w_glu, b_glu):
    b_, t_, _ = z.shape
    tt = min(SSM_TSTEP, t_)
    n = SSM_N
    full = lambda shape: pl.BlockSpec(shape, lambda b, i: (0,) * len(shape))
    return pl.pallas_call(
        _s5_kernel,
        out_shape=jax.ShapeDtypeStruct((b_, t_, SSM_W), F32),
        grid=(b_, t_ // tt),
        in_specs=[
            pl.BlockSpec((pl.Element(1), pl.Element(tt), pl.Element(SSM_W)), lambda b, i: (b, i * tt, SSM_COL0)),
            full((SSM_W, 2 * n)), full((2, SUBLANES, n)), full((3, SUBLANES, 2 * n)), full((2 * n, SSM_W)),
            full((1, SSM_W)), full((SSM_W, SSM_W)), full((1, SSM_W)),
        ],
        out_specs=pl.BlockSpec((1, tt, SSM_W), lambda b, i: (b, i, 0)),
        scratch_shapes=[pltpu.VMEM((tt, 2 * n), F32), pltpu.VMEM((2, n), F32)],
        compiler_params=_params(("parallel", "arbitrary")),
        name="s5",
    )(z, bmat, pw, step, cmat, d_skip, w_glu, b_glu)


def _s5_discretise(a_re, a_im, log_dt, b_re, b_im, c_re, c_im):
    g_, n_, c_ = SSM_GROUPS, SSM_STATE, SSM_GROUP
    dt = jnp.exp(log_dt)[:, None]
    mag = jnp.exp(dt * a_re)
    abar_r, abar_i = mag * jnp.cos(dt * a_im), mag * jnp.sin(dt * a_im)
    den = a_re * a_re + a_im * a_im
    nr, ni = abar_r - 1.0, abar_i
    coef_r, coef_i = (nr * a_re + ni * a_im) / den, (ni * a_re - nr * a_im) / den
    bbar_r = coef_r[..., None] * b_re - coef_i[..., None] * b_im
    bbar_i = coef_r[..., None] * b_im + coef_i[..., None] * b_re
    eye = jnp.eye(g_, dtype=F32)
    bd_in = lambda m: jnp.einsum('gnc,gh->gchn', m, eye).reshape(g_ * c_, g_ * n_)
    bmat = jnp.concatenate([bd_in(bbar_r), bd_in(bbar_i)], axis=1).astype(BF16)
    bd_out = lambda m: jnp.einsum('gcn,gh->gnhc', m, eye).reshape(g_ * n_, g_ * c_)
    cmat = jnp.concatenate([bd_out(c_re), -bd_out(c_im)], axis=0).astype(BF16)
    ar, ai = abar_r.reshape(1, -1), abar_i.reshape(1, -1)
    pows_r, pows_i = [ar], [ai]
    for _ in range(SUBLANES - 1):
        pr, pi = pows_r[-1], pows_i[-1]
        pows_r.append(pr * ar - pi * ai)
        pows_i.append(pr * ai + pi * ar)
    pw = jnp.stack([jnp.concatenate(pows_r, axis=0), jnp.concatenate(pows_i, axis=0)])
    t_idx = jnp.arange(SUBLANES)[:, None]
    step = jnp.stack([jnp.where(t_idx >= d, jnp.concatenate([pows_r[d - 1], pows_i[d - 1]], axis=1), 0.0)
                      for d in (1, 2, 4)])
    return bmat, pw, step, cmat


def _outproj_kernel(x_ref, yrw_ref, yda_ref, yss_ref, w1_ref, w2_ref, w3_ref, g_ref, o_ref):
    y = (_dot(yrw_ref[...].astype(BF16), w1_ref[...]) + _dot(yda_ref[...].astype(BF16), w2_ref[...])
         + _dot(yss_ref[...].astype(BF16), w3_ref[...]))
    o_ref[...] = x_ref[...] + _rms(y, g_ref[...], NORM_EPS)


def _outproj(x, y_rw, y_da, y_ss, w, g):
    m, d = x.shape
    tm = min(ROW_TILE, m)
    rowblk = lambda n: pl.BlockSpec((tm, n), lambda i: (i, 0))
    assert RW_W == DA_W and (RW_W + DA_W) % SSM_W == 0
    band = lambda rows, idx: pl.BlockSpec((rows, d), lambda i: (idx, 0))
    return pl.pallas_call(
        _outproj_kernel,
        out_shape=jax.ShapeDtypeStruct((m, d), F32),
        grid=(m // tm,),
        in_specs=[rowblk(d), rowblk(RW_W), rowblk(DA_W), rowblk(SSM_W),
                  band(RW_W, 0), band(DA_W, 1), band(SSM_W, (RW_W + DA_W) // SSM_W),
                  pl.BlockSpec((1, d), lambda i: (0, 0))],
        out_specs=rowblk(d),
        compiler_params=_params(("parallel",)),
        name="outproj",
    )(x, y_rw, y_da, y_ss, w, w, w, g)


def kernel(x, ffn1_pre_g, ffn1_w_gu, ffn1_w_down, ffn1_post_g, mix_pre_g, w_in, rw_mu, rw_w0, rw_w2, rw_a0, rw_a2, rw_g2, rw_k_k, rw_k_a, rw_r_k, rw_gn_w, rw_gn_b, da_lq1, da_lk1, da_lq2, da_lk2, da_subln_w, ssm_a_re, ssm_a_im, ssm_log_dt, ssm_b_re, ssm_b_im, ssm_c_re, ssm_c_im, ssm_d, ssm_w_glu, ssm_b_glu, w_out, mix_post_g, ffn2_pre_g, ffn2_w_gu, ffn2_w_down, ffn2_post_g):
    b_, t_, d_ = x.shape
    m = b_ * t_
    row = lambda a: a.reshape(1, -1)
    xf = x.reshape(m, d_)
    for l in range(DEPTH):
        xf, xn = _ffn(xf, row(ffn1_pre_g[l]), ffn1_w_gu, ffn1_w_down, row(ffn1_post_g[l]), l, row(mix_pre_g[l]))

        z = _inproj(xn, w_in, l).reshape(b_, t_, IN_COLS)

        maps = _rwchunk(z, row(rw_mu[l]), row(rw_w0[l]), rw_w2[l], row(rw_a0[l]), rw_a2[l], rw_g2[l],
                        row(rw_k_k[l]), row(rw_k_a[l]), row(rw_r_k[l]))
        y_rw = _rwstate(*maps, row(rw_gn_w[l]), row(rw_gn_b[l]))

        lam_init = 0.8 - 0.6 * math.exp(-0.3 * l)
        y_da = _attn(z, row(da_lq1[l]), row(da_lk1[l]), row(da_lq2[l]), row(da_lk2[l]),
                     row(da_subln_w[l]), lam_init)

        bmat, pw, step, cmat = _s5_discretise(ssm_a_re[l], ssm_a_im[l], ssm_log_dt[l], ssm_b_re[l],
                                              ssm_b_im[l], ssm_c_re[l], ssm_c_im[l])
        y_ss = _s5(z, bmat, pw, step, cmat, row(ssm_d[l]), ssm_w_glu[l].astype(BF16), row(ssm_b_glu[l]))

        xf = _outproj(xf, y_rw.reshape(m, RW_W), y_da.reshape(m, DA_W), y_ss.reshape(m, SSM_W),
                      w_out[l].astype(BF16), row(mix_post_g[l]))

        xf, = _ffn(xf, row(ffn2_pre_g[l]), ffn2_w_gu, ffn2_w_down, row(ffn2_post_g[l]), l)
    return xf.reshape(b_, t_, d_)
```

```python
import functools
import math

import jax
import jax.numpy as jnp
from jax import lax
from jax.experimental import pallas as pl
from jax.experimental.pallas import tpu as pltpu

F32 = jnp.float32
BF16 = jnp.bfloat16

D_MODEL = 2048
DEPTH = 2
RW_HEADS = 12
RW_HD = 64
RW_W = RW_HEADS * RW_HD
RW_DECAY_LORA = 64
RW_AAA_LORA = 64
RW_GATE_LORA = 128
RW_GN_EPS = 64e-5
DA_HEADS = 6
DA_HD = 64
DA_W = DA_HEADS * 2 * DA_HD
DA_SUBLN_EPS = 1e-5
SSM_W = D_MODEL - RW_W - DA_W
SSM_GROUP = 16
SSM_GROUPS = SSM_W // SSM_GROUP
SSM_STATE = 64
SSM_N = SSM_GROUPS * SSM_STATE
RW_COLS = 3 * RW_W + RW_DECAY_LORA + RW_AAA_LORA + RW_GATE_LORA
DA_COLS = 3 * DA_W
D_FF = 5504
NORM_EPS = 1e-6

LANES = 128
SUBLANES = 8
VMEM_LIMIT = 56 * 1024 * 1024
FF_TILE = 256
FFN_ROW_TILE = 1024
IN_ROW_TILE = 1024
IN_COLS = RW_COLS + DA_COLS + SSM_W
IN_TILE = IN_COLS // 3
SSM_COL0 = RW_COLS + DA_COLS
ROW_TILE = 512
RW_CHUNK = 64
RW_TSTEP = 512
RW_STATE_HEADS = 4
RW_MAP_HEADS = 4
RW_MAP_TSTEP = 512
ATT_TILE = 512
SSM_TSTEP = 512
NEG_BIG = -1e30


def _dot(a, b):
    return jnp.dot(a, b, preferred_element_type=F32)


def _dot_nt(a, b):
    return lax.dot_general(a, b, (((1,), (1,)), ((), ())), preferred_element_type=F32)


def _rms(x, g, eps):
    return x * lax.rsqrt(jnp.mean(x * x, axis=-1, keepdims=True) + eps) * g


def _params(sem):
    return pltpu.CompilerParams(dimension_semantics=sem, vmem_limit_bytes=VMEM_LIMIT)


def _ff_offset(j, base=0):
    return (base // LANES + jnp.minimum(j * (FF_TILE // LANES), (D_FF - FF_TILE) // LANES)) * LANES


def _ffn_kernel(*refs, emit_next):
    if emit_next:
        x_ref, pre_ref, wg_ref, wu_ref, wd_ref, post_ref, ng_ref, o_ref, nx_ref, xn_ref = refs
    else:
        x_ref, pre_ref, wg_ref, wu_ref, wd_ref, post_ref, o_ref, xn_ref = refs
    j = pl.program_id(1)

    @pl.when(j == 0)
    def _():
        xn_ref[...] = _rms(x_ref[...], pre_ref[...], NORM_EPS).astype(BF16)
        o_ref[...] = jnp.zeros_like(o_ref)

    xn = xn_ref[...]
    gate = _dot(xn, wg_ref[0].astype(BF16))
    up = _dot(xn, wu_ref[0].astype(BF16))
    h = gate * jax.nn.sigmoid(gate) * up
    unit = _ff_offset(j) + lax.broadcasted_iota(jnp.int32, h.shape, 1)
    h = jnp.where(unit >= j * FF_TILE, h, 0.0).astype(BF16)
    o_ref[...] += _dot(h, wd_ref[0].astype(BF16))

    @pl.when(j == pl.num_programs(1) - 1)
    def _():
        y = x_ref[...] + 0.5 * _rms(o_ref[...], post_ref[...], NORM_EPS)
        o_ref[...] = y
        if emit_next:
            nx_ref[...] = _rms(y, ng_ref[...], NORM_EPS).astype(BF16)


def _ffn(x, pre_g, w_gu, w_down, post_g, layer, next_g=None):
    m, d = x.shape
    tm, tf = min(FFN_ROW_TILE, m), FF_TILE
    once = pl.Buffered(1)
    emit_next = next_g is not None
    vec = pl.BlockSpec((1, d), lambda i, j: (0, 0))
    row_out = pl.BlockSpec((tm, d), lambda i, j: (i, 0), pipeline_mode=once)
    return pl.pallas_call(
        functools.partial(_ffn_kernel, emit_next=emit_next),
        out_shape=[jax.ShapeDtypeStruct((m, d), F32)] + [jax.ShapeDtypeStruct((m, d), BF16)] * emit_next,
        grid=(m // tm, pl.cdiv(D_FF, tf)),
        in_specs=[
            pl.BlockSpec((tm, d), lambda i, j: (i, 0)),
            vec,
            pl.BlockSpec((pl.Element(1), pl.Element(d), pl.Element(tf)), lambda i, j: (layer, 0, _ff_offset(j))),
            pl.BlockSpec((pl.Element(1), pl.Element(d), pl.Element(tf)), lambda i, j: (layer, 0, _ff_offset(j, D_FF))),
            pl.BlockSpec((pl.Element(1), pl.Element(tf), pl.Element(d)), lambda i, j: (layer, _ff_offset(j), 0)),
            vec,
        ] + [vec] * emit_next,
        out_specs=[row_out] * (1 + emit_next),
        scratch_shapes=[pltpu.VMEM((tm, d), BF16)],
        compiler_params=_params(("parallel", "arbitrary")),
        name="ffn",
    )(x, pre_g, w_gu, w_gu, w_down, post_g, *([next_g] * emit_next))


def _inproj_kernel(xn_ref, w_ref, o_ref, wb_ref):
    @pl.when(pl.program_id(1) == 0)
    def _():
        wb_ref[...] = w_ref[0].astype(BF16)

    o_ref[...] = _dot(xn_ref[...], wb_ref[...])


def _inproj(xn, w_in, layer):
    m, d = xn.shape
    tm, tn = min(IN_ROW_TILE, m), IN_TILE
    return pl.pallas_call(
        _inproj_kernel,
        out_shape=jax.ShapeDtypeStruct((m, IN_COLS), F32),
        grid=(IN_COLS // tn, m // tm),
        in_specs=[
            pl.BlockSpec((tm, d), lambda j, i: (i, 0)),
            pl.BlockSpec((pl.Element(1), pl.Element(d), pl.Element(tn)), lambda j, i: (layer, 0, j * tn),
                         pipeline_mode=pl.Buffered(1)),
        ],
        out_specs=pl.BlockSpec((tm, tn), lambda j, i: (i, j)),
        scratch_shapes=[pltpu.VMEM((d, tn), BF16)],
        compiler_params=_params(("parallel", "arbitrary")),
        name="inproj",
    )(xn, w_in)


def _softplus(x):
    return jnp.maximum(x, 0.0) + jnp.log(1.0 + jnp.exp(-jnp.abs(x)))


def _head_sums(x, first):
    s_first = jnp.sum(jnp.where(first, x, 0.0), axis=-1, keepdims=True)
    s_all = jnp.sum(x, axis=-1, keepdims=True)
    return jnp.where(first, s_first, s_all - s_first)


_NN = ((1,), (0,))
_NT = ((1,), (1,))
_TN = ((0,), (0,))


def _bdot(a, b, dims=_NN):
    return lax.dot_general(a.astype(BF16), b.astype(BF16), (dims, ((), ())), preferred_element_type=F32)


def _split_dot(x, w):
    xh, wh = x.astype(BF16), w.astype(BF16)
    xl, wl = (x - xh.astype(F32)).astype(BF16), (w - wh.astype(F32)).astype(BF16)
    return _dot(jnp.concatenate([xh, xl, xh], axis=1), jnp.concatenate([wh, wh, wl], axis=0))


def _unit_lower_inverses(ns, rowc, colc, bd):
    c = ns[0].shape[0]
    eye = (rowc == colc).astype(F32)
    base = SUBLANES
    diag = (rowc // base) == (colc // base)
    a0 = [jnp.where(diag, n, 0.0) for n in ns]
    a2 = [_bdot(a, bd(a)) for a in a0]
    a4 = [_bdot(a, bd(a)) for a in a2]
    xs = [_bdot(eye + p, bd(eye + q)) for p, q in zip(a0, a2)]
    xs = [_bdot(p, bd(eye + q)) for p, q in zip(xs, a4)]
    m = base
    while m < c:
        off = ((rowc // (2 * m)) == (colc // (2 * m))) & ((rowc // m) != (colc // m))
        xe = [_bdot(p, bd(jnp.where(off, n, 0.0))) for p, n in zip(xs, ns)]
        xs = [p + _bdot(q, bd(p)) for p, q in zip(xs, xe)]
        m *= 2
    return xs


def _rwchunk_kernel(zr_ref, zk_ref, zv_ref, zl_ref, pr_ref, pk_ref, pv_ref, pl_ref,
                    mur_ref, muk_ref, muv_ref, mul_ref, w0_ref, w2_ref, a0_ref, a2_ref, g2_ref, kk_ref, ka_ref, rk_ref,
                    rp_o, y0_o, tm_o, sadd_o, bonus_o, g_o):
    c, kd = RW_CHUNK, RW_HD
    tstep, gw = zr_ref.shape[1:]
    nheads, nchunks = gw // kd, tstep // c
    first_row = lax.broadcasted_iota(jnp.int32, (tstep, gw), 0) == 0
    at_start = pl.program_id(2) == 0

    def token_shift(z_ref, prev_ref, mu_ref):
        z = z_ref[0]
        before = jnp.where(at_start, 0.0, prev_ref[0, SUBLANES - 1:SUBLANES, :])
        z_prev = jnp.where(first_row, before, pltpu.roll(z, 1, 0))
        return z + (z_prev - z) * mu_ref[...]

    r = token_shift(zr_ref, pr_ref, mur_ref)
    k = token_shift(zk_ref, pk_ref, muk_ref)
    v = token_shift(zv_ref, pv_ref, muv_ref)
    lora = token_shift(zl_ref, pl_ref, mul_ref)
    o1, o2 = RW_DECAY_LORA, RW_DECAY_LORA + RW_AAA_LORA
    w = -_softplus(-(w0_ref[...] + _split_dot(jnp.tanh(lora[:, :o1]), w2_ref[...]))) - 0.5
    lw = -jnp.exp(w)
    a = jax.nn.sigmoid(a0_ref[...] + _split_dot(lora[:, o1:o2], a2_ref[...]))
    g_o[0] = _split_dot(jax.nn.sigmoid(lora[:, o2:]), g2_ref[...])
    kk = k * kk_ref[...]
    k = k * (1.0 + (a - 1.0) * ka_ref[...])
    first = lax.broadcasted_iota(jnp.int32, (tstep, LANES), 1) < kd
    blocks = [slice(j * LANES, (j + 1) * LANES) for j in range(gw // LANES)]
    norm = jnp.concatenate([jnp.sqrt(_head_sums(kk[:, s] * kk[:, s], first)) for s in blocks], axis=1)
    kk = kk / jnp.maximum(norm, 1e-12)
    b = kk * a
    rkr = r * k * rk_ref[...]
    bonus_o[0] = jnp.concatenate([_head_sums(rkr[:, s], first) for s in blocks], axis=1) * v

    rowc = lax.broadcasted_iota(jnp.int32, (c, gw), 0)
    lane_head = lax.broadcasted_iota(jnp.int32, (c, gw), 1) // kd
    colc = lax.broadcasted_iota(jnp.int32, (c, gw), 1) - lane_head * kd
    strict = rowc > colc
    incl = rowc >= colc
    own_block = (lax.broadcasted_iota(jnp.int32, (gw, gw), 0) // kd) == (lax.broadcasted_iota(jnp.int32, (gw, gw), 1) // kd)

    def bd(x):
        return jnp.where(own_block, jnp.concatenate([x.astype(BF16)] * nheads, axis=0), 0.0)

    def own_blocks(p):
        return sum(jnp.where(lane_head == hh, p[hh * kd:(hh + 1) * kd, :], 0.0) for hh in range(nheads))

    tri = (lax.broadcasted_iota(jnp.int32, (c, c), 0) >= lax.broadcasted_iota(jnp.int32, (c, c), 1)).astype(BF16)
    chunk_rows = [slice(ci * c, (ci + 1) * c) for ci in range(nchunks)]
    lw_c = [lw[rows] for rows in chunk_rows]
    lw_hi = [x.astype(BF16) for x in lw_c]
    lc_hi = [_bdot(tri, h) for h in lw_hi]
    lc_lo = [_bdot(tri, x - h.astype(F32)) for x, h in zip(lw_c, lw_hi)]
    lc = [p + q for p, q in zip(lc_hi, lc_lo)]
    e_inc = [jnp.exp(x) for x in lc]
    e_inv = [jnp.exp(-x) for x in lc]
    e_end = [jnp.exp(x[c - 1:c, :] - x) for x in lc]
    at = [-kk[rows] * jnp.exp(x - w_) for rows, x, w_ in zip(chunk_rows, lc, lw_c)]
    rt = [r[rows] * e for rows, e in zip(chunk_rows, e_inc)]
    vc = [v[rows] for rows in chunk_rows]
    ar = [jnp.concatenate([p, q], axis=0) for p, q in zip(at, rt)]
    pb = [_bdot(p, bd(b[rows] * e), _NT) for p, rows, e in zip(ar, chunk_rows, e_inv)]
    pk = [_bdot(p, bd(k[rows] * e), _NT) for p, rows, e in zip(ar, chunk_rows, e_inv)]
    a_ab = [jnp.where(strict, p[:c], 0.0) for p in pb]
    r_b = [jnp.where(incl, p[c:], 0.0) for p in pb]
    a_ak = [jnp.where(strict, p[:c], 0.0) for p in pk]
    r_k = [jnp.where(incl, p[c:], 0.0) for p in pk]
    akv = [_bdot(jnp.concatenate([p, q], axis=0), bd(x)) for p, q, x in zip(a_ak, r_k, vc)]
    kev = [own_blocks(_bdot(k[rows] * e, x, _TN)) for rows, e, x in zip(chunk_rows, e_end, vc)]
    xs = _unit_lower_inverses(a_ab, rowc, colc, bd)
    za = [_bdot(x, bd(p)) for x, p in zip(xs, at)]
    zv = [_bdot(x, bd(p[:c])) for x, p in zip(xs, akv)]
    rwa = [_bdot(p, bd(q)) for p, q in zip(r_b, za)]
    rwv = [_bdot(p, bd(q)) for p, q in zip(r_b, zv)]
    tz = [_bdot(b[rows] * e, jnp.concatenate([p, q], axis=1), _TN)
          for rows, e, p, q in zip(chunk_rows, e_end, za, zv)]
    for ci in range(nchunks):
        rows = pl.ds(ci * c, c)
        rp_o[0, rows, :] = rt[ci] + rwa[ci]
        y0_o[0, rows, :] = rwv[ci] + akv[ci][c:]
        tm_o[0, rows, :] = (jnp.where(rowc == colc, jnp.broadcast_to(e_inc[ci][c - 1:c, :], (c, gw)), 0.0)
                            + own_blocks(tz[ci][:, :gw]))
        sadd_o[0, rows, :] = own_blocks(tz[ci][:, gw:]) + kev[ci]


def _rwchunk(z, mu, w0, w2, a0, a2, g2, k_k, k_a, r_k):
    b_, t_, _ = z.shape
    gw = RW_MAP_HEADS * RW_HD
    assert RW_CHUNK == RW_HD
    assert RW_DECAY_LORA + RW_AAA_LORA + RW_GATE_LORA == gw and RW_W % gw == 0
    tstep = min(RW_MAP_TSTEP, t_)
    nprev = tstep // SUBLANES
    col_r, col_k, col_v, col_l = 0, RW_W // gw, 2 * RW_W // gw, 3 * RW_W // gw
    seq = lambda c0, own: pl.BlockSpec((1, tstep, gw), lambda bb, h, i: (bb, i, c0 + own * h))
    prev = lambda c0, own: pl.BlockSpec(
        (1, SUBLANES, gw), lambda bb, h, i: (bb, jnp.maximum(i * nprev - 1, 0), c0 + own * h))
    vec = lambda c0, own: pl.BlockSpec((1, gw), lambda bb, h, i: (0, c0 + own * h))
    mat = lambda rows: pl.BlockSpec((rows, gw), lambda bb, h, i: (0, h))
    places = [(col_r, 1), (col_k, 1), (col_v, 1), (col_l, 0)]
    out = pl.BlockSpec((1, tstep, gw), lambda bb, h, i: (bb, i, h))
    return pl.pallas_call(
        _rwchunk_kernel,
        out_shape=[jax.ShapeDtypeStruct((b_, t_, RW_W), F32)] * 6,
        grid=(b_, RW_W // gw, t_ // tstep),
        in_specs=([seq(*p) for p in places] + [prev(*p) for p in places] + [vec(*p) for p in places]
                  + [vec(0, 1), mat(RW_DECAY_LORA), vec(0, 1), mat(RW_AAA_LORA), mat(RW_GATE_LORA),
                     vec(0, 1), vec(0, 1), vec(0, 1)]),
        out_specs=[out] * 6,
        compiler_params=_params(("parallel", "parallel", "parallel")),
        name="rwkv_chunk",
    )(z, z, z, z, z, z, z, z, mu, mu, mu, mu, w0, w2, a0, a2, g2, k_k, k_a, r_k)


def _rwstate_kernel(rp_ref, y0_ref, tm_ref, sadd_ref, bonus_ref, g_ref, gnw_ref, gnb_ref, o_ref, s_ref, y_ref):
    @pl.when(pl.program_id(1) == 0)
    def _():
        s_ref[...] = jnp.zeros_like(s_ref)

    c, kd = RW_CHUNK, RW_HD
    tstep, width = y_ref.shape
    gw = RW_STATE_HEADS * kd
    own_block = (lax.broadcasted_iota(jnp.int32, (gw, gw), 0) // kd) == (lax.broadcasted_iota(jnp.int32, (gw, gw), 1) // kd)
    for ci in range(tstep // c):
        rows = pl.ds(ci * c, c)
        for gi in range(width // gw):
            lanes = slice(gi * gw, (gi + 1) * gw)
            st = s_ref[:, lanes].astype(BF16)
            st_bd = jnp.where(own_block, jnp.concatenate([st] * RW_STATE_HEADS, axis=0), 0.0)
            lhs = jnp.concatenate([tm_ref[0, rows, lanes], rp_ref[0, rows, lanes]], axis=0).astype(BF16)
            res = _dot(lhs, st_bd)
            s_ref[:, lanes] = res[:c] + sadd_ref[0, rows, lanes]
            y_ref[rows, lanes] = res[c:] + y0_ref[0, rows, lanes]

    first = lax.broadcasted_iota(jnp.int32, (tstep, LANES), 1) < kd
    head_mean = lambda x: _head_sums(x, first) * (1.0 / kd)
    for j in range(width // LANES):
        lanes = slice(j * LANES, (j + 1) * LANES)
        y = y_ref[:, lanes]
        d = y - head_mean(y)
        yn = d * lax.rsqrt(head_mean(d * d) + RW_GN_EPS) * gnw_ref[:, lanes] + gnb_ref[:, lanes]
        o_ref[0, :, lanes] = (yn + bonus_ref[0, :, lanes]) * g_ref[0, :, lanes]


def _rwstate(rp, y0, tm, sadd, bonus, g, gn_w, gn_b):
    b_, t_, w_ = rp.shape
    tstep = min(RW_TSTEP, t_)
    seq = pl.BlockSpec((1, tstep, w_), lambda bb, i: (bb, i, 0))
    par = pl.BlockSpec((1, w_), lambda bb, i: (0, 0))
    return pl.pallas_call(
        _rwstate_kernel,
        out_shape=jax.ShapeDtypeStruct((b_, t_, w_), F32),
        grid=(b_, t_ // tstep),
        in_specs=[seq] * 6 + [par] * 2,
        out_specs=seq,
        scratch_shapes=[pltpu.VMEM((RW_HD, w_), F32), pltpu.VMEM((tstep, w_), F32)],
        compiler_params=_params(("parallel", "arbitrary")),
        name="rwkv_state",
    )(rp, y0, tm, sadd, bonus, g, gn_w, gn_b)


def _attn_kernel(q_ref, k_ref, v_ref, lq1_ref, lk1_ref, lq2_ref, lk2_ref, sw_ref, o_ref,
                 qs_ref, m_ref, acc_ref, *, lam_init):
    qi = pl.program_id(2)
    t = q_ref.shape[1]
    q = q_ref[0] * (DA_HD ** -0.5)
    lane = lax.broadcasted_iota(jnp.int32, q.shape, 1)
    qs_ref[0] = jnp.where(lane < DA_HD, q, 0.0).astype(BF16)
    qs_ref[1] = jnp.where(lane >= DA_HD, q, 0.0).astype(BF16)
    m_ref[...] = jnp.full_like(m_ref, NEG_BIG)
    acc_ref[...] = jnp.zeros_like(acc_ref)
    hw = q.shape[1]
    ones = jnp.ones((t, hw), BF16)
    subheads = range(2)

    def key_tiles(js, diagonal):
        kbs, vbs = [], []
        for j in js:
            rows = pl.ds(pl.multiple_of(j * t, t), t)
            kbs.append(k_ref[0, rows, :].astype(BF16))
            vbs.append(jnp.concatenate([v_ref[0, rows, :].astype(BF16), ones], axis=1))
        s = [[_dot_nt(qs_ref[c], kb) for kb in kbs] for c in subheads]
        if diagonal:
            causal = (lax.broadcasted_iota(jnp.int32, (t, t), 1) <= lax.broadcasted_iota(jnp.int32, (t, t), 0))
            for c in subheads:
                s[c][-1] = jnp.where(causal, s[c][-1], NEG_BIG)
        m_old = [m_ref[c] for c in subheads]
        m_new = []
        for c in subheads:
            m = m_old[c]
            for x in s[c]:
                m = jnp.maximum(m, jnp.max(x, axis=-1, keepdims=True))
            m_new.append(m)
        m_wide = [jnp.concatenate([m] * (t // hw), axis=1) for m in m_new]
        p = [[jnp.exp(x - m_wide[c]).astype(BF16) for x in s[c]] for c in subheads]
        pv = [sum(_dot(x, vb) for x, vb in zip(p[c], vbs)) for c in subheads]
        for c in subheads:
            alpha = jnp.exp(m_old[c] - m_new[c])
            acc_ref[c] = jnp.concatenate([alpha, alpha], axis=1) * acc_ref[c] + pv[c]
            m_ref[c] = m_new[c]

    def pair_below_diagonal(jj, carry):
        key_tiles([2 * jj, 2 * jj + 1], False)
        return carry

    lax.fori_loop(0, qi // 2, pair_below_diagonal, 0)

    @pl.when(qi % 2 == 1)
    def _():
        key_tiles([qi - 1, qi], True)

    @pl.when(qi % 2 == 0)
    def _():
        key_tiles([qi], True)

    lam = (jnp.exp(jnp.sum(lq1_ref[...] * lk1_ref[...], axis=-1, keepdims=True))
           - jnp.exp(jnp.sum(lq2_ref[...] * lk2_ref[...], axis=-1, keepdims=True)) + lam_init)
    o = acc_ref[0, :, :hw] / acc_ref[0, :, hw:] - lam * (acc_ref[1, :, :hw] / acc_ref[1, :, hw:])
    o = o * lax.rsqrt(jnp.mean(o * o, axis=-1, keepdims=True) + DA_SUBLN_EPS) * sw_ref[...]
    o_ref[0] = o * (1.0 - lam_init)


def _attn(z, lq1, lk1, lq2, lk2, subln_w, lam_init):
    b_, t_, _ = z.shape
    tq = min(ATT_TILE, t_)
    hw = 2 * DA_HD
    q0, k0, v0 = RW_COLS // hw, (RW_COLS + DA_W) // hw, (RW_COLS + 2 * DA_W) // hw
    small = lambda n: pl.BlockSpec((1, n), lambda b, h, qi: (0, 0))
    return pl.pallas_call(
        functools.partial(_attn_kernel, lam_init=lam_init),
        out_shape=jax.ShapeDtypeStruct((b_, t_, DA_W), F32),
        grid=(b_, DA_HEADS, t_ // tq),
        in_specs=[
            pl.BlockSpec((1, tq, hw), lambda b, h, qi: (b, qi, q0 + h)),
            pl.BlockSpec((1, t_, hw), lambda b, h, qi: (b, 0, k0 + h)),
            pl.BlockSpec((1, t_, hw), lambda b, h, qi: (b, 0, v0 + h)),
            small(DA_HD), small(DA_HD), small(DA_HD), small(DA_HD), small(hw),
        ],
        out_specs=pl.BlockSpec((1, tq, hw), lambda b, h, qi: (b, qi, h)),
        scratch_shapes=[pltpu.VMEM((2, tq, hw), BF16), pltpu.VMEM((2, tq, hw), F32),
                        pltpu.VMEM((2, tq, 2 * hw), F32)],
        compiler_params=_params(("parallel", "parallel", "arbitrary")),
        name="diff_attn",
    )(z, z, z, lq1, lk1, lq2, lk2, subln_w)


def _gelu_tanh(x):
    return 0.5 * x * (1.0 + jnp.tanh(math.sqrt(2.0 / math.pi) * (x + 0.044715 * (x * x * x))))


def _s5_kernel(u_ref, bmat_ref, pw_ref, step_ref, cmat_ref, d_ref, wglu_ref, bglu_ref, o_ref,
               xs_ref, carry_ref):
    @pl.when(pl.program_id(1) == 0)
    def _():
        carry_ref[...] = jnp.zeros_like(carry_ref)

    u = u_ref[0]
    tt = u.shape[0]
    n = SSM_N
    hw, hn = SSM_W // 2, n // 2
    ub = u.astype(BF16)
    for h in range(2):
        ch = slice(h * hw, (h + 1) * hw)
        for part in (0, n):
            st = slice(part + h * hn, part + (h + 1) * hn)
            xs_ref[:, st] = _dot(ub[:, ch], bmat_ref[ch, st])
    pr, pi = pw_ref[0], pw_ref[1]

    def block(i, carry):
        cr, ci = carry
        rows = pl.ds(pl.multiple_of(i * SUBLANES, SUBLANES), SUBLANES)
        xr = xs_ref[rows, :n]
        xi = xs_ref[rows, n:]
        for lvl, d in enumerate((1, 2, 4)):
            ar, ai = step_ref[lvl, :, :n], step_ref[lvl, :, n:]
            sr, si = pltpu.roll(xr, d, 0), pltpu.roll(xi, d, 0)
            xr, xi = xr + ar * sr - ai * si, xi + ar * si + ai * sr
        xr, xi = xr + pr * cr - pi * ci, xi + pr * ci + pi * cr
        xs_ref[rows, :n] = xr
        xs_ref[rows, n:] = xi
        return xr[SUBLANES - 1:, :], xi[SUBLANES - 1:, :]

    cr, ci = lax.fori_loop(0, tt // SUBLANES, block, (carry_ref[0:1, :], carry_ref[1:2, :]))
    carry_ref[0:1, :] = cr
    carry_ref[1:2, :] = ci
    ys = []
    for h in range(2):
        ch = slice(h * hw, (h + 1) * hw)
        re, im = slice(h * hn, (h + 1) * hn), slice(n + h * hn, n + (h + 1) * hn)
        ys.append(_dot(xs_ref[:, re].astype(BF16), cmat_ref[re, ch]) + _dot(xs_ref[:, im].astype(BF16), cmat_ref[im, ch]))
    y = jnp.concatenate(ys, axis=1) + d_ref[...] * u
    y = _gelu_tanh(y)
    o_ref[0] = y * jax.nn.sigmoid(_dot(y.astype(BF16), wglu_ref[...]) + bglu_ref[...])


def _s5(z, bmat, pw, step, cmat, d_skip, w_glu, b_glu):
    b_, t_, _ = z.shape
    tt = min(SSM_TSTEP, t_)
    n = SSM_N
    full = lambda shape: pl.BlockSpec(shape, lambda b, i: (0,) * len(shape))
    return pl.pallas_call(
        _s5_kernel,
        out_shape=jax.ShapeDtypeStruct((b_, t_, SSM_W), F32),
        grid=(b_, t_ // tt),
        in_specs=[
            pl.BlockSpec((pl.Element(1), pl.Element(tt), pl.Element(SSM_W)), lambda b, i: (b, i * tt, SSM_COL0)),
            full((SSM_W, 2 * n)), full((2, SUBLANES, n)), full((3, SUBLANES, 2 * n)), full((2 * n, SSM_W)),
            full((1, SSM_W)), full((SSM_W, SSM_W)), full((1, SSM_W)),
        ],
        out_specs=pl.BlockSpec((1, tt, SSM_W), lambda b, i: (b, i, 0)),
        scratch_shapes=[pltpu.VMEM((tt, 2 * n), F32), pltpu.VMEM((2, n), F32)],
        compiler_params=_params(("parallel", "arbitrary")),
        name="s5",
    )(z, bmat, pw, step, cmat, d_skip, w_glu, b_glu)


def _s5_discretise(a_re, a_im, log_dt, b_re, b_im, c_re, c_im):
    g_, n_, c_ = SSM_GROUPS, SSM_STATE, SSM_GROUP
    dt = jnp.exp(log_dt)[:, None]
    mag = jnp.exp(dt * a_re)
    abar_r, abar_i = mag * jnp.cos(dt * a_im), mag * jnp.sin(dt * a_im)
    den = a_re * a_re + a_im * a_im
    nr, ni = abar_r - 1.0, abar_i
    coef_r, coef_i = (nr * a_re + ni * a_im) / den, (ni * a_re - nr * a_im) / den
    bbar_r = coef_r[..., None] * b_re - coef_i[..., None] * b_im
    bbar_i = coef_r[..., None] * b_im + coef_i[..., None] * b_re
    eye = jnp.eye(g_, dtype=F32)
    bd_in = lambda m: jnp.einsum('gnc,gh->gchn', m, eye).reshape(g_ * c_, g_ * n_)
    bmat = jnp.concatenate([bd_in(bbar_r), bd_in(bbar_i)], axis=1).astype(BF16)
    bd_out = lambda m: jnp.einsum('gcn,gh->gnhc', m, eye).reshape(g_ * n_, g_ * c_)
    cmat = jnp.concatenate([bd_out(c_re), -bd_out(c_im)], axis=0).astype(BF16)
    ar, ai = abar_r.reshape(1, -1), abar_i.reshape(1, -1)
    pows_r, pows_i = [ar], [ai]
    for _ in range(SUBLANES - 1):
        pr, pi = pows_r[-1], pows_i[-1]
        pows_r.append(pr * ar - pi * ai)
        pows_i.append(pr * ai + pi * ar)
    pw = jnp.stack([jnp.concatenate(pows_r, axis=0), jnp.concatenate(pows_i, axis=0)])
    t_idx = jnp.arange(SUBLANES)[:, None]
    step = jnp.stack([jnp.where(t_idx >= d, jnp.concatenate([pows_r[d - 1], pows_i[d - 1]], axis=1), 0.0)
                      for d in (1, 2, 4)])
    return bmat, pw, step, cmat


def _outproj_kernel(x_ref, yrw_ref, yda_ref, yss_ref, w1_ref, w2_ref, w3_ref, g_ref, o_ref):
    y = (_dot(yrw_ref[...].astype(BF16), w1_ref[...]) + _dot(yda_ref[...].astype(BF16), w2_ref[...])
         + _dot(yss_ref[...].astype(BF16), w3_ref[...]))
    o_ref[...] = x_ref[...] + _rms(y, g_ref[...], NORM_EPS)


def _outproj(x, y_rw, y_da, y_ss, w, g):
    m, d = x.shape
    tm = min(ROW_TILE, m)
    rowblk = lambda n: pl.BlockSpec((tm, n), lambda i: (i, 0))
    assert RW_W == DA_W and (RW_W + DA_W) % SSM_W == 0
    band = lambda rows, idx: pl.BlockSpec((rows, d), lambda i: (idx, 0))
    return pl.pallas_call(
        _outproj_kernel,
        out_shape=jax.ShapeDtypeStruct((m, d), F32),
        grid=(m // tm,),
        in_specs=[rowblk(d), rowblk(RW_W), rowblk(DA_W), rowblk(SSM_W),
                  band(RW_W, 0), band(DA_W, 1), band(SSM_W, (RW_W + DA_W) // SSM_W),
                  pl.BlockSpec((1, d), lambda i: (0, 0))],
        out_specs=rowblk(d),
        compiler_params=_params(("parallel",)),
        name="outproj",
    )(x, y_rw, y_da, y_ss, w, w, w, g)


def kernel(x, ffn1_pre_g, ffn1_w_gu, ffn1_w_down, ffn1_post_g, mix_pre_g, w_in, rw_mu, rw_w0, rw_w2, rw_a0, rw_a2, rw_g2, rw_k_k, rw_k_a, rw_r_k, rw_gn_w, rw_gn_b, da_lq1, da_lk1, da_lq2, da_lk2, da_subln_w, ssm_a_re, ssm_a_im, ssm_log_dt, ssm_b_re, ssm_b_im, ssm_c_re, ssm_c_im, ssm_d, ssm_w_glu, ssm_b_glu, w_out, mix_post_g, ffn2_pre_g, ffn2_w_gu, ffn2_w_down, ffn2_post_g):
    b_, t_, d_ = x.shape
    m = b_ * t_
    row = lambda a: a.reshape(1, -1)
    xf = x.reshape(m, d_)
    for l in range(DEPTH):
        xf, xn = _ffn(xf, row(ffn1_pre_g[l]), ffn1_w_gu, ffn1_w_down, row(ffn1_post_g[l]), l, row(mix_pre_g[l]))

        z = _inproj(xn, w_in, l).reshape(b_, t_, IN_COLS)

        maps = _rwchunk(z, row(rw_mu[l]), row(rw_w0[l]), rw_w2[l], row(rw_a0[l]), rw_a2[l], rw_g2[l],
                        row(rw_k_k[l]), row(rw_k_a[l]), row(rw_r_k[l]))
        y_rw = _rwstate(*maps, row(rw_gn_w[l]), row(rw_gn_b[l]))

        lam_init = 0.8 - 0.6 * math.exp(-0.3 * l)
        y_da = _attn(z, row(da_lq1[l]), row(da_lk1[l]), row(da_lq2[l]), row(da_lk2[l]),
                     row(da_subln_w[l]), lam_init)

        bmat, pw, step, cmat = _s5_discretise(ssm_a_re[l], ssm_a_im[l], ssm_log_dt[l], ssm_b_re[l],
                                              ssm_b_im[l], ssm_c_re[l], ssm_c_im[l])
        y_ss = _s5(z, bmat, pw, step, cmat, row(ssm_d[l]), ssm_w_glu[l].astype(BF16), row(ssm_b_glu[l]))

        xf = _outproj(xf, y_rw.reshape(m, RW_W), y_da.reshape(m, DA_W), y_ss.reshape(m, SSM_W),
                      w_out[l].astype(BF16), row(mix_post_g[l]))

        xf, = _ffn(xf, row(ffn2_pre_g[l]), ffn2_w_gu, ffn2_w_down, row(ffn2_post_g[l]), l)
    return xf.reshape(b_, t_, d_)
```

```python
import functools
import math

import jax
import jax.numpy as jnp
from jax import lax
from jax.experimental import pallas as pl
from jax.experimental.pallas import tpu as pltpu

F32 = jnp.float32
BF16 = jnp.bfloat16

D_MODEL = 2048
DEPTH = 2
RW_HEADS = 12
RW_HD = 64
RW_W = RW_HEADS * RW_HD
RW_DECAY_LORA = 64
RW_AAA_LORA = 64
RW_GATE_LORA = 128
RW_GN_EPS = 64e-5
DA_HEADS = 6
DA_HD = 64
DA_W = DA_HEADS * 2 * DA_HD
DA_SUBLN_EPS = 1e-5
SSM_W = D_MODEL - RW_W - DA_W
SSM_GROUP = 16
SSM_GROUPS = SSM_W // SSM_GROUP
SSM_STATE = 64
SSM_N = SSM_GROUPS * SSM_STATE
RW_COLS = 3 * RW_W + RW_DECAY_LORA + RW_AAA_LORA + RW_GATE_LORA
DA_COLS = 3 * DA_W
D_FF = 5504
NORM_EPS = 1e-6

LANES = 128
SUBLANES = 8
VMEM_LIMIT = 56 * 1024 * 1024
FFN_VMEM_LIMIT = 60 * 1024 * 1024
FF_TILE = 256
FFN_ROW_TILE = 1024
IN_ROW_TILE = 1024
IN_COLS = RW_COLS + DA_COLS + SSM_W
IN_TILE = IN_COLS // 3
SSM_COL0 = RW_COLS + DA_COLS
ROW_TILE = 512
RW_CHUNK = 64
RW_TSTEP = 512
RW_STATE_HEADS = 4
RW_MAP_HEADS = 4
RW_MAP_TSTEP = 1024
ATT_TILE = 512
SSM_TSTEP = 512
NEG_BIG = -1e30


def _dot(a, b):
    return jnp.dot(a, b, preferred_element_type=F32)


def _dot_nt(a, b):
    return lax.dot_general(a, b, (((1,), (1,)), ((), ())), preferred_element_type=F32)


def _rms(x, g, eps):
    return x * lax.rsqrt(jnp.mean(x * x, axis=-1, keepdims=True) + eps) * g


def _params(sem):
    return pltpu.CompilerParams(dimension_semantics=sem, vmem_limit_bytes=VMEM_LIMIT)


def _ff_offset(j, base=0):
    return (base // LANES + jnp.minimum(j * (FF_TILE // LANES), (D_FF - FF_TILE) // LANES)) * LANES


def _ffn_kernel(*refs, emit_next):
    if emit_next:
        x_ref, pre_ref, wg_ref, wu_ref, wd_ref, post_ref, ng_ref, o_ref, nx_ref, xn_ref = refs
    else:
        x_ref, pre_ref, wg_ref, wu_ref, wd_ref, post_ref, o_ref, xn_ref = refs
    j = pl.program_id(1)

    @pl.when(j == 0)
    def _():
        xn_ref[...] = _rms(x_ref[...], pre_ref[...], NORM_EPS).astype(BF16)
        o_ref[...] = jnp.zeros_like(o_ref)

    xn = xn_ref[...]
    gate = _dot(xn, wg_ref[0].astype(BF16))
    up = _dot(xn, wu_ref[0].astype(BF16))
    h = gate * jax.nn.sigmoid(gate) * up
    unit = _ff_offset(j) + lax.broadcasted_iota(jnp.int32, h.shape, 1)
    h = jnp.where(unit >= j * FF_TILE, h, 0.0).astype(BF16)
    o_ref[...] += _dot(h, wd_ref[0].astype(BF16))

    @pl.when(j == pl.num_programs(1) - 1)
    def _():
        y = x_ref[...] + 0.5 * _rms(o_ref[...], post_ref[...], NORM_EPS)
        o_ref[...] = y
        if emit_next:
            nx_ref[...] = _rms(y, ng_ref[...], NORM_EPS).astype(BF16)


def _ffn(x, pre_g, w_gu, w_down, post_g, layer, next_g=None):
    m, d = x.shape
    tm, tf = min(FFN_ROW_TILE, m), FF_TILE
    emit_next = next_g is not None
    vec = pl.BlockSpec((1, d), lambda i, j: (0, 0))
    row_out = pl.BlockSpec((tm, d), lambda i, j: (i, 0), pipeline_mode=pl.Buffered(1 if emit_next else 2))
    return pl.pallas_call(
        functools.partial(_ffn_kernel, emit_next=emit_next),
        out_shape=[jax.ShapeDtypeStruct((m, d), F32)] + [jax.ShapeDtypeStruct((m, d), BF16)] * emit_next,
        grid=(m // tm, pl.cdiv(D_FF, tf)),
        in_specs=[
            pl.BlockSpec((tm, d), lambda i, j: (i, 0)),
            vec,
            pl.BlockSpec((pl.Element(1), pl.Element(d), pl.Element(tf)), lambda i, j: (layer, 0, _ff_offset(j))),
            pl.BlockSpec((pl.Element(1), pl.Element(d), pl.Element(tf)), lambda i, j: (layer, 0, _ff_offset(j, D_FF))),
            pl.BlockSpec((pl.Element(1), pl.Element(tf), pl.Element(d)), lambda i, j: (layer, _ff_offset(j), 0)),
            vec,
        ] + [vec] * emit_next,
        out_specs=[row_out] * (1 + emit_next),
        scratch_shapes=[pltpu.VMEM((tm, d), BF16)],
        compiler_params=pltpu.CompilerParams(dimension_semantics=("parallel", "arbitrary"),
                                             vmem_limit_bytes=FFN_VMEM_LIMIT),
        name="ffn",
    )(x, pre_g, w_gu, w_gu, w_down, post_g, *([next_g] * emit_next))


def _inproj_kernel(xn_ref, w_ref, o_ref, wb_ref):
    @pl.when(pl.program_id(1) == 0)
    def _():
        wb_ref[...] = w_ref[0].astype(BF16)

    o_ref[...] = _dot(xn_ref[...], wb_ref[...])


def _inproj(xn, w_in, layer):
    m, d = xn.shape
    tm, tn = min(IN_ROW_TILE, m), IN_TILE
    return pl.pallas_call(
        _inproj_kernel,
        out_shape=jax.ShapeDtypeStruct((m, IN_COLS), F32),
        grid=(IN_COLS // tn, m // tm),
        in_specs=[
            pl.BlockSpec((tm, d), lambda j, i: (i, 0)),
            pl.BlockSpec((pl.Element(1), pl.Element(d), pl.Element(tn)), lambda j, i: (layer, 0, j * tn),
                         pipeline_mode=pl.Buffered(1)),
        ],
        out_specs=pl.BlockSpec((tm, tn), lambda j, i: (i, j)),
        scratch_shapes=[pltpu.VMEM((d, tn), BF16)],
        compiler_params=_params(("parallel", "arbitrary")),
        name="inproj",
    )(xn, w_in)


def _softplus(x):
    return jnp.maximum(x, 0.0) + jnp.log(1.0 + jnp.exp(-jnp.abs(x)))


def _head_sums(x, first):
    s_first = jnp.sum(jnp.where(first, x, 0.0), axis=-1, keepdims=True)
    s_all = jnp.sum(x, axis=-1, keepdims=True)
    return jnp.where(first, s_first, s_all - s_first)


_NN = ((1,), (0,))
_NT = ((1,), (1,))
_TN = ((0,), (0,))


def _bdot(a, b, dims=_NN):
    return lax.dot_general(a.astype(BF16), b.astype(BF16), (dims, ((), ())), preferred_element_type=F32)


def _split_dot(x, w):
    xh, wh = x.astype(BF16), w.astype(BF16)
    xl, wl = (x - xh.astype(F32)).astype(BF16), (w - wh.astype(F32)).astype(BF16)
    return _dot(jnp.concatenate([xh, xl, xh], axis=1), jnp.concatenate([wh, wh, wl], axis=0))


def _unit_lower_inverses(ns, rowc, colc, bd):
    c = ns[0].shape[0]
    eye = (rowc == colc).astype(F32)
    base = SUBLANES
    diag = (rowc // base) == (colc // base)
    a0 = [jnp.where(diag, n, 0.0) for n in ns]
    a2 = [_bdot(a, bd(a)) for a in a0]
    a4 = [_bdot(a, bd(a)) for a in a2]
    xs = [_bdot(eye + p, bd(eye + q)) for p, q in zip(a0, a2)]
    xs = [_bdot(p, bd(eye + q)) for p, q in zip(xs, a4)]
    m = base
    while m < c:
        off = ((rowc // (2 * m)) == (colc // (2 * m))) & ((rowc // m) != (colc // m))
        xe = [_bdot(p, bd(jnp.where(off, n, 0.0))) for p, n in zip(xs, ns)]
        xs = [p + _bdot(q, bd(p)) for p, q in zip(xs, xe)]
        m *= 2
    return xs


def _rwchunk_kernel(zr_ref, zk_ref, zv_ref, zl_ref, pr_ref, pk_ref, pv_ref, pl_ref,
                    mur_ref, muk_ref, muv_ref, mul_ref, w0_ref, w2_ref, a0_ref, a2_ref, g2_ref, kk_ref, ka_ref, rk_ref,
                    rp_o, y0_o, tm_o, sadd_o, bonus_o, g_o):
    c, kd = RW_CHUNK, RW_HD
    tstep, gw = zr_ref.shape[1:]
    nheads, nchunks = gw // kd, tstep // c
    first_row = lax.broadcasted_iota(jnp.int32, (tstep, gw), 0) == 0
    at_start = pl.program_id(2) == 0

    def token_shift(z_ref, prev_ref, mu_ref):
        z = z_ref[0]
        before = jnp.where(at_start, 0.0, prev_ref[0, SUBLANES - 1:SUBLANES, :])
        z_prev = jnp.where(first_row, before, pltpu.roll(z, 1, 0))
        return z + (z_prev - z) * mu_ref[...]

    r = token_shift(zr_ref, pr_ref, mur_ref)
    k = token_shift(zk_ref, pk_ref, muk_ref)
    v = token_shift(zv_ref, pv_ref, muv_ref)
    lora = token_shift(zl_ref, pl_ref, mul_ref)
    o1, o2 = RW_DECAY_LORA, RW_DECAY_LORA + RW_AAA_LORA
    w = -_softplus(-(w0_ref[...] + _split_dot(jnp.tanh(lora[:, :o1]), w2_ref[...]))) - 0.5
    lw = -jnp.exp(w)
    a = jax.nn.sigmoid(a0_ref[...] + _split_dot(lora[:, o1:o2], a2_ref[...]))
    g_o[0] = _split_dot(jax.nn.sigmoid(lora[:, o2:]), g2_ref[...])
    kk = k * kk_ref[...]
    k = k * (1.0 + (a - 1.0) * ka_ref[...])
    first = lax.broadcasted_iota(jnp.int32, (tstep, LANES), 1) < kd
    blocks = [slice(j * LANES, (j + 1) * LANES) for j in range(gw // LANES)]
    norm = jnp.concatenate([jnp.sqrt(_head_sums(kk[:, s] * kk[:, s], first)) for s in blocks], axis=1)
    kk = kk / jnp.maximum(norm, 1e-12)
    b = kk * a
    rkr = r * k * rk_ref[...]
    bonus_o[0] = jnp.concatenate([_head_sums(rkr[:, s], first) for s in blocks], axis=1) * v

    rowc = lax.broadcasted_iota(jnp.int32, (c, gw), 0)
    lane_head = lax.broadcasted_iota(jnp.int32, (c, gw), 1) // kd
    colc = lax.broadcasted_iota(jnp.int32, (c, gw), 1) - lane_head * kd
    strict = rowc > colc
    incl = rowc >= colc
    own_block = (lax.broadcasted_iota(jnp.int32, (gw, gw), 0) // kd) == (lax.broadcasted_iota(jnp.int32, (gw, gw), 1) // kd)

    def bd(x):
        return jnp.where(own_block, jnp.concatenate([x.astype(BF16)] * nheads, axis=0), 0.0)

    def own_blocks(p):
        return sum(jnp.where(lane_head == hh, p[hh * kd:(hh + 1) * kd, :], 0.0) for hh in range(nheads))

    tri = (lax.broadcasted_iota(jnp.int32, (c, c), 0) >= lax.broadcasted_iota(jnp.int32, (c, c), 1)).astype(BF16)
    chunk_rows = [slice(ci * c, (ci + 1) * c) for ci in range(nchunks)]
    lw_c = [lw[rows] for rows in chunk_rows]
    lw_hi = [x.astype(BF16) for x in lw_c]
    lc_hi = [_bdot(tri, h) for h in lw_hi]
    lc_lo = [_bdot(tri, x - h.astype(F32)) for x, h in zip(lw_c, lw_hi)]
    lc = [p + q for p, q in zip(lc_hi, lc_lo)]
    e_inc = [jnp.exp(x) for x in lc]
    e_inv = [jnp.exp(-x) for x in lc]
    e_end = [jnp.exp(x[c - 1:c, :] - x) for x in lc]
    at = [-kk[rows] * jnp.exp(x - w_) for rows, x, w_ in zip(chunk_rows, lc, lw_c)]
    rt = [r[rows] * e for rows, e in zip(chunk_rows, e_inc)]
    vc = [v[rows] for rows in chunk_rows]
    ar = [jnp.concatenate([p, q], axis=0) for p, q in zip(at, rt)]
    pb = [_bdot(p, bd(b[rows] * e), _NT) for p, rows, e in zip(ar, chunk_rows, e_inv)]
    pk = [_bdot(p, bd(k[rows] * e), _NT) for p, rows, e in zip(ar, chunk_rows, e_inv)]
    a_ab = [jnp.where(strict, p[:c], 0.0) for p in pb]
    r_b = [jnp.where(incl, p[c:], 0.0) for p in pb]
    a_ak = [jnp.where(strict, p[:c], 0.0) for p in pk]
    r_k = [jnp.where(incl, p[c:], 0.0) for p in pk]
    akv = [_bdot(jnp.concatenate([p, q], axis=0), bd(x)) for p, q, x in zip(a_ak, r_k, vc)]
    kev = [own_blocks(_bdot(k[rows] * e, x, _TN)) for rows, e, x in zip(chunk_rows, e_end, vc)]
    xs = _unit_lower_inverses(a_ab, rowc, colc, bd)
    za = [_bdot(x, bd(p)) for x, p in zip(xs, at)]
    zv = [_bdot(x, bd(p[:c])) for x, p in zip(xs, akv)]
    rwa = [_bdot(p, bd(q)) for p, q in zip(r_b, za)]
    rwv = [_bdot(p, bd(q)) for p, q in zip(r_b, zv)]
    tz = [_bdot(b[rows] * e, jnp.concatenate([p, q], axis=1), _TN)
          for rows, e, p, q in zip(chunk_rows, e_end, za, zv)]
    for ci in range(nchunks):
        rows = pl.ds(ci * c, c)
        rp_o[0, rows, :] = rt[ci] + rwa[ci]
        y0_o[0, rows, :] = rwv[ci] + akv[ci][c:]
        tm_o[0, rows, :] = (jnp.where(rowc == colc, jnp.broadcast_to(e_inc[ci][c - 1:c, :], (c, gw)), 0.0)
                            + own_blocks(tz[ci][:, :gw]))
        sadd_o[0, rows, :] = own_blocks(tz[ci][:, gw:]) + kev[ci]


def _rwchunk(z, mu, w0, w2, a0, a2, g2, k_k, k_a, r_k):
    b_, t_, _ = z.shape
    gw = RW_MAP_HEADS * RW_HD
    assert RW_CHUNK == RW_HD
    assert RW_DECAY_LORA + RW_AAA_LORA + RW_GATE_LORA == gw and RW_W % gw == 0
    tstep = min(RW_MAP_TSTEP, t_)
    nprev = tstep // SUBLANES
    col_r, col_k, col_v, col_l = 0, RW_W // gw, 2 * RW_W // gw, 3 * RW_W // gw
    seq = lambda c0, own: pl.BlockSpec((1, tstep, gw), lambda bb, h, i: (bb, i, c0 + own * h))
    prev = lambda c0, own: pl.BlockSpec(
        (1, SUBLANES, gw), lambda bb, h, i: (bb, jnp.maximum(i * nprev - 1, 0), c0 + own * h))
    vec = lambda c0, own: pl.BlockSpec((1, gw), lambda bb, h, i: (0, c0 + own * h))
    mat = lambda rows: pl.BlockSpec((rows, gw), lambda bb, h, i: (0, h))
    places = [(col_r, 1), (col_k, 1), (col_v, 1), (col_l, 0)]
    out = pl.BlockSpec((1, tstep, gw), lambda bb, h, i: (bb, i, h))
    return pl.pallas_call(
        _rwchunk_kernel,
        out_shape=[jax.ShapeDtypeStruct((b_, t_, RW_W), F32)] * 6,
        grid=(b_, RW_W // gw, t_ // tstep),
        in_specs=([seq(*p) for p in places] + [prev(*p) for p in places] + [vec(*p) for p in places]
                  + [vec(0, 1), mat(RW_DECAY_LORA), vec(0, 1), mat(RW_AAA_LORA), mat(RW_GATE_LORA),
                     vec(0, 1), vec(0, 1), vec(0, 1)]),
        out_specs=[out] * 6,
        compiler_params=_params(("parallel", "parallel", "parallel")),
        name="rwkv_chunk",
    )(z, z, z, z, z, z, z, z, mu, mu, mu, mu, w0, w2, a0, a2, g2, k_k, k_a, r_k)


def _rwstate_kernel(rp_ref, y0_ref, tm_ref, sadd_ref, bonus_ref, g_ref, gnw_ref, gnb_ref, o_ref, s_ref, y_ref):
    @pl.when(pl.program_id(1) == 0)
    def _():
        s_ref[...] = jnp.zeros_like(s_ref)

    c, kd = RW_CHUNK, RW_HD
    tstep, width = y_ref.shape
    gw = RW_STATE_HEADS * kd
    own_block = (lax.broadcasted_iota(jnp.int32, (gw, gw), 0) // kd) == (lax.broadcasted_iota(jnp.int32, (gw, gw), 1) // kd)
    for ci in range(tstep // c):
        rows = pl.ds(ci * c, c)
        for gi in range(width // gw):
            lanes = slice(gi * gw, (gi + 1) * gw)
            st = s_ref[:, lanes].astype(BF16)
            st_bd = jnp.where(own_block, jnp.concatenate([st] * RW_STATE_HEADS, axis=0), 0.0)
            lhs = jnp.concatenate([tm_ref[0, rows, lanes], rp_ref[0, rows, lanes]], axis=0).astype(BF16)
            res = _dot(lhs, st_bd)
            s_ref[:, lanes] = res[:c] + sadd_ref[0, rows, lanes]
            y_ref[rows, lanes] = res[c:] + y0_ref[0, rows, lanes]

    first = lax.broadcasted_iota(jnp.int32, (tstep, LANES), 1) < kd
    head_mean = lambda x: _head_sums(x, first) * (1.0 / kd)
    for j in range(width // LANES):
        lanes = slice(j * LANES, (j + 1) * LANES)
        y = y_ref[:, lanes]
        d = y - head_mean(y)
        yn = d * lax.rsqrt(head_mean(d * d) + RW_GN_EPS) * gnw_ref[:, lanes] + gnb_ref[:, lanes]
        o_ref[0, :, lanes] = (yn + bonus_ref[0, :, lanes]) * g_ref[0, :, lanes]


def _rwstate(rp, y0, tm, sadd, bonus, g, gn_w, gn_b):
    b_, t_, w_ = rp.shape
    tstep = min(RW_TSTEP, t_)
    seq = pl.BlockSpec((1, tstep, w_), lambda bb, i: (bb, i, 0))
    par = pl.BlockSpec((1, w_), lambda bb, i: (0, 0))
    return pl.pallas_call(
        _rwstate_kernel,
        out_shape=jax.ShapeDtypeStruct((b_, t_, w_), F32),
        grid=(b_, t_ // tstep),
        in_specs=[seq] * 6 + [par] * 2,
        out_specs=seq,
        scratch_shapes=[pltpu.VMEM((RW_HD, w_), F32), pltpu.VMEM((tstep, w_), F32)],
        compiler_params=_params(("parallel", "arbitrary")),
        name="rwkv_state",
    )(rp, y0, tm, sadd, bonus, g, gn_w, gn_b)


def _attn_kernel(q_ref, k_ref, v_ref, lq1_ref, lk1_ref, lq2_ref, lk2_ref, sw_ref, o_ref,
                 qs_ref, m_ref, acc_ref, *, lam_init):
    qi = pl.program_id(2)
    t = q_ref.shape[1]
    q = q_ref[0] * (DA_HD ** -0.5)
    lane = lax.broadcasted_iota(jnp.int32, q.shape, 1)
    qs_ref[0] = jnp.where(lane < DA_HD, q, 0.0).astype(BF16)
    qs_ref[1] = jnp.where(lane >= DA_HD, q, 0.0).astype(BF16)
    m_ref[...] = jnp.full_like(m_ref, NEG_BIG)
    acc_ref[...] = jnp.zeros_like(acc_ref)
    hw = q.shape[1]
    ones = jnp.ones((t, hw), BF16)
    subheads = range(2)

    def key_tiles(js, diagonal):
        kbs, vbs = [], []
        for j in js:
            rows = pl.ds(pl.multiple_of(j * t, t), t)
            kbs.append(k_ref[0, rows, :].astype(BF16))
            vbs.append(jnp.concatenate([v_ref[0, rows, :].astype(BF16), ones], axis=1))
        s = [[_dot_nt(qs_ref[c], kb) for kb in kbs] for c in subheads]
        if diagonal:
            causal = (lax.broadcasted_iota(jnp.int32, (t, t), 1) <= lax.broadcasted_iota(jnp.int32, (t, t), 0))
            for c in subheads:
                s[c][-1] = jnp.where(causal, s[c][-1], NEG_BIG)
        m_old = [m_ref[c] for c in subheads]
        m_new = []
        for c in subheads:
            m = m_old[c]
            for x in s[c]:
                m = jnp.maximum(m, jnp.max(x, axis=-1, keepdims=True))
            m_new.append(m)
        m_wide = [jnp.concatenate([m] * (t // hw), axis=1) for m in m_new]
        p = [[jnp.exp(x - m_wide[c]).astype(BF16) for x in s[c]] for c in subheads]
        pv = [sum(_dot(x, vb) for x, vb in zip(p[c], vbs)) for c in subheads]
        for c in subheads:
            alpha = jnp.exp(m_old[c] - m_new[c])
            acc_ref[c] = jnp.concatenate([alpha, alpha], axis=1) * acc_ref[c] + pv[c]
            m_ref[c] = m_new[c]

    def pair_below_diagonal(jj, carry):
        key_tiles([2 * jj, 2 * jj + 1], False)
        return carry

    lax.fori_loop(0, qi // 2, pair_below_diagonal, 0)

    @pl.when(qi % 2 == 1)
    def _():
        key_tiles([qi - 1, qi], True)

    @pl.when(qi % 2 == 0)
    def _():
        key_tiles([qi], True)

    lam = (jnp.exp(jnp.sum(lq1_ref[...] * lk1_ref[...], axis=-1, keepdims=True))
           - jnp.exp(jnp.sum(lq2_ref[...] * lk2_ref[...], axis=-1, keepdims=True)) + lam_init)
    o = acc_ref[0, :, :hw] / acc_ref[0, :, hw:] - lam * (acc_ref[1, :, :hw] / acc_ref[1, :, hw:])
    o = o * lax.rsqrt(jnp.mean(o * o, axis=-1, keepdims=True) + DA_SUBLN_EPS) * sw_ref[...]
    o_ref[0] = o * (1.0 - lam_init)


def _attn(z, lq1, lk1, lq2, lk2, subln_w, lam_init):
    b_, t_, _ = z.shape
    tq = min(ATT_TILE, t_)
    hw = 2 * DA_HD
    q0, k0, v0 = RW_COLS // hw, (RW_COLS + DA_W) // hw, (RW_COLS + 2 * DA_W) // hw
    small = lambda n: pl.BlockSpec((1, n), lambda b, h, qi: (0, 0))
    return pl.pallas_call(
        functools.partial(_attn_kernel, lam_init=lam_init),
        out_shape=jax.ShapeDtypeStruct((b_, t_, DA_W), F32),
        grid=(b_, DA_HEADS, t_ // tq),
        in_specs=[
            pl.BlockSpec((1, tq, hw), lambda b, h, qi: (b, qi, q0 + h)),
            pl.BlockSpec((1, t_, hw), lambda b, h, qi: (b, 0, k0 + h)),
            pl.BlockSpec((1, t_, hw), lambda b, h, qi: (b, 0, v0 + h)),
            small(DA_HD), small(DA_HD), small(DA_HD), small(DA_HD), small(hw),
        ],
        out_specs=pl.BlockSpec((1, tq, hw), lambda b, h, qi: (b, qi, h)),
        scratch_shapes=[pltpu.VMEM((2, tq, hw), BF16), pltpu.VMEM((2, tq, hw), F32),
                        pltpu.VMEM((2, tq, 2 * hw), F32)],
        compiler_params=_params(("parallel", "parallel", "arbitrary")),
        name="diff_attn",
    )(z, z, z, lq1, lk1, lq2, lk2, subln_w)


def _gelu_tanh(x):
    return 0.5 * x * (1.0 + jnp.tanh(math.sqrt(2.0 / math.pi) * (x + 0.044715 * (x * x * x))))


def _s5_kernel(u_ref, bmat_ref, pw_ref, step_ref, cmat_ref, d_ref, wglu_ref, bglu_ref, o_ref,
               xs_ref, carry_ref):
    @pl.when(pl.program_id(1) == 0)
    def _():
        carry_ref[...] = jnp.zeros_like(carry_ref)

    u = u_ref[0]
    tt = u.shape[0]
    n = SSM_N
    hw, hn = SSM_W // 2, n // 2
    ub = u.astype(BF16)
    for h in range(2):
        ch = slice(h * hw, (h + 1) * hw)
        for part in (0, n):
            st = slice(part + h * hn, part + (h + 1) * hn)
            xs_ref[:, st] = _dot(ub[:, ch], bmat_ref[ch, st])
    pr, pi = pw_ref[0], pw_ref[1]

    def block(i, carry):
        cr, ci = carry
        rows = pl.ds(pl.multiple_of(i * SUBLANES, SUBLANES), SUBLANES)
        xr = xs_ref[rows, :n]
        xi = xs_ref[rows, n:]
        for lvl, d in enumerate((1, 2, 4)):
            ar, ai = step_ref[lvl, :, :n], step_ref[lvl, :, n:]
            sr, si = pltpu.roll(xr, d, 0), pltpu.roll(xi, d, 0)
            xr, xi = xr + ar * sr - ai * si, xi + ar * si + ai * sr
        xr, xi = xr + pr * cr - pi * ci, xi + pr * ci + pi * cr
        xs_ref[rows, :n] = xr
        xs_ref[rows, n:] = xi
        return xr[SUBLANES - 1:, :], xi[SUBLANES - 1:, :]

    cr, ci = lax.fori_loop(0, tt // SUBLANES, block, (carry_ref[0:1, :], carry_ref[1:2, :]))
    carry_ref[0:1, :] = cr
    carry_ref[1:2, :] = ci
    ys = []
    for h in range(2):
        ch = slice(h * hw, (h + 1) * hw)
        re, im = slice(h * hn, (h + 1) * hn), slice(n + h * hn, n + (h + 1) * hn)
        ys.append(_dot(xs_ref[:, re].astype(BF16), cmat_ref[re, ch]) + _dot(xs_ref[:, im].astype(BF16), cmat_ref[im, ch]))
    y = jnp.concatenate(ys, axis=1) + d_ref[...] * u
    y = _gelu_tanh(y)
    o_ref[0] = y * jax.nn.sigmoid(_dot(y.astype(BF16), wglu_ref[...]) + bglu_ref[...])


def _s5(z, bmat, pw, step, cmat, d_skip, w_glu, b_glu):
    b_, t_, _ = z.shape
    tt = min(SSM_TSTEP, t_)
    n = SSM_N
    full = lambda shape: pl.BlockSpec(shape, lambda b, i: (0,) * len(shape))
    return pl.pallas_call(
        _s5_kernel,
        out_shape=jax.ShapeDtypeStruct((b_, t_, SSM_W), F32),
        grid=(b_, t_ // tt),
        in_specs=[
            pl.BlockSpec((pl.Element(1), pl.Element(tt), pl.Element(SSM_W)), lambda b, i: (b, i * tt, SSM_COL0)),
            full((SSM_W, 2 * n)), full((2, SUBLANES, n)), full((3, SUBLANES, 2 * n)), full((2 * n, SSM_W)),
            full((1, SSM_W)), full((SSM_W, SSM_W)), full((1, SSM_W)),
        ],
        out_specs=pl.BlockSpec((1, tt, SSM_W), lambda b, i: (b, i, 0)),
        scratch_shapes=[pltpu.VMEM((tt, 2 * n), F32), pltpu.VMEM((2, n), F32)],
        compiler_params=_params(("parallel", "arbitrary")),
        name="s5",
    )(z, bmat, pw, step, cmat, d_skip, w_glu, b_glu)


def _s5_discretise(a_re, a_im, log_dt, b_re, b_im, c_re, c_im):
    g_, n_, c_ = SSM_GROUPS, SSM_STATE, SSM_GROUP
    dt = jnp.exp(log_dt)[:, None]
    mag = jnp.exp(dt * a_re)
    abar_r, abar_i = mag * jnp.cos(dt * a_im), mag * jnp.sin(dt * a_im)
    den = a_re * a_re + a_im * a_im
    nr, ni = abar_r - 1.0, abar_i
    coef_r, coef_i = (nr * a_re + ni * a_im) / den, (ni * a_re - nr * a_im) / den
    bbar_r = coef_r[..., None] * b_re - coef_i[..., None] * b_im
    bbar_i = coef_r[..., None] * b_im + coef_i[..., None] * b_re
    eye = jnp.eye(g_, dtype=F32)
    bd_in = lambda m: jnp.einsum('gnc,gh->gchn', m, eye).reshape(g_ * c_, g_ * n_)
    bmat = jnp.concatenate([bd_in(bbar_r), bd_in(bbar_i)], axis=1).astype(BF16)
    bd_out = lambda m: jnp.einsum('gcn,gh->gnhc', m, eye).reshape(g_ * n_, g_ * c_)
    cmat = jnp.concatenate([bd_out(c_re), -bd_out(c_im)], axis=0).astype(BF16)
    ar, ai = abar_r.reshape(1, -1), abar_i.reshape(1, -1)
    pows_r, pows_i = [ar], [ai]
    for _ in range(SUBLANES - 1):
        pr, pi = pows_r[-1], pows_i[-1]
        pows_r.append(pr * ar - pi * ai)
        pows_i.append(pr * ai + pi * ar)
    pw = jnp.stack([jnp.concatenate(pows_r, axis=0), jnp.concatenate(pows_i, axis=0)])
    t_idx = jnp.arange(SUBLANES)[:, None]
    step = jnp.stack([jnp.where(t_idx >= d, jnp.concatenate([pows_r[d - 1], pows_i[d - 1]], axis=1), 0.0)
                      for d in (1, 2, 4)])
    return bmat, pw, step, cmat


def _outproj_kernel(x_ref, yrw_ref, yda_ref, yss_ref, w1_ref, w2_ref, w3_ref, g_ref, o_ref):
    y = (_dot(yrw_ref[...].astype(BF16), w1_ref[...]) + _dot(yda_ref[...].astype(BF16), w2_ref[...])
         + _dot(yss_ref[...].astype(BF16), w3_ref[...]))
    o_ref[...] = x_ref[...] + _rms(y, g_ref[...], NORM_EPS)


def _outproj(x, y_rw, y_da, y_ss, w, g):
    m, d = x.shape
    tm = min(ROW_TILE, m)
    rowblk = lambda n: pl.BlockSpec((tm, n), lambda i: (i, 0))
    assert RW_W == DA_W and (RW_W + DA_W) % SSM_W == 0
    band = lambda rows, idx: pl.BlockSpec((rows, d), lambda i: (idx, 0))
    return pl.pallas_call(
        _outproj_kernel,
        out_shape=jax.ShapeDtypeStruct((m, d), F32),
        grid=(m // tm,),
        in_specs=[rowblk(d), rowblk(RW_W), rowblk(DA_W), rowblk(SSM_W),
                  band(RW_W, 0), band(DA_W, 1), band(SSM_W, (RW_W + DA_W) // SSM_W),
                  pl.BlockSpec((1, d), lambda i: (0, 0))],
        out_specs=rowblk(d),
        compiler_params=_params(("parallel",)),
        name="outproj",
    )(x, y_rw, y_da, y_ss, w, w, w, g)


def kernel(x, ffn1_pre_g, ffn1_w_gu, ffn1_w_down, ffn1_post_g, mix_pre_g, w_in, rw_mu, rw_w0, rw_w2, rw_a0, rw_a2, rw_g2, rw_k_k, rw_k_a, rw_r_k, rw_gn_w, rw_gn_b, da_lq1, da_lk1, da_lq2, da_lk2, da_subln_w, ssm_a_re, ssm_a_im, ssm_log_dt, ssm_b_re, ssm_b_im, ssm_c_re, ssm_c_im, ssm_d, ssm_w_glu, ssm_b_glu, w_out, mix_post_g, ffn2_pre_g, ffn2_w_gu, ffn2_w_down, ffn2_post_g):
    b_, t_, d_ = x.shape
    m = b_ * t_
    row = lambda a: a.reshape(1, -1)
    xf = x.reshape(m, d_)
    for l in range(DEPTH):
        xf, xn = _ffn(xf, row(ffn1_pre_g[l]), ffn1_w_gu, ffn1_w_down, row(ffn1_post_g[l]), l, row(mix_pre_g[l]))

        z = _inproj(xn, w_in, l).reshape(b_, t_, IN_COLS)

        maps = _rwchunk(z, row(rw_mu[l]), row(rw_w0[l]), rw_w2[l], row(rw_a0[l]), rw_a2[l], rw_g2[l],
                        row(rw_k_k[l]), row(rw_k_a[l]), row(rw_r_k[l]))
        y_rw = _rwstate(*maps, row(rw_gn_w[l]), row(rw_gn_b[l]))

        lam_init = 0.8 - 0.6 * math.exp(-0.3 * l)
        y_da = _attn(z, row(da_lq1[l]), row(da_lk1[l]), row(da_lq2[l]), row(da_lk2[l]),
                     row(da_subln_w[l]), lam_init)

        bmat, pw, step, cmat = _s5_discretise(ssm_a_re[l], ssm_a_im[l], ssm_log_dt[l], ssm_b_re[l],
                                              ssm_b_im[l], ssm_c_re[l], ssm_c_im[l])
        y_ss = _s5(z, bmat, pw, step, cmat, row(ssm_d[l]), ssm_w_glu[l].astype(BF16), row(ssm_b_glu[l]))

        xf = _outproj(xf, y_rw.reshape(m, RW_W), y_da.reshape(m, DA_W), y_ss.reshape(m, SSM_W),
                      w_out[l].astype(BF16), row(mix_post_g[l]))

        xf, = _ffn(xf, row(ffn2_pre_g[l]), ffn2_w_gu, ffn2_w_down, row(ffn2_post_g[l]), l)
    return xf.reshape(b_, t_, d_)
```

```python
import functools
import math

import jax
import jax.numpy as jnp
from jax import lax
from jax.experimental import pallas as pl
from jax.experimental.pallas import tpu as pltpu

F32 = jnp.float32
BF16 = jnp.bfloat16

D_MODEL = 2048
DEPTH = 2
RW_HEADS = 12
RW_HD = 64
RW_W = RW_HEADS * RW_HD
RW_DECAY_LORA = 64
RW_AAA_LORA = 64
RW_GATE_LORA = 128
RW_GN_EPS = 64e-5
DA_HEADS = 6
DA_HD = 64
DA_W = DA_HEADS * 2 * DA_HD
DA_SUBLN_EPS = 1e-5
SSM_W = D_MODEL - RW_W - DA_W
SSM_GROUP = 16
SSM_GROUPS = SSM_W // SSM_GROUP
SSM_STATE = 64
SSM_N = SSM_GROUPS * SSM_STATE
RW_COLS = 3 * RW_W + RW_DECAY_LORA + RW_AAA_LORA + RW_GATE_LORA
DA_COLS = 3 * DA_W
D_FF = 5504
NORM_EPS = 1e-6

LANES = 128
SUBLANES = 8
VMEM_LIMIT = 56 * 1024 * 1024
FFN_VMEM_LIMIT = 60 * 1024 * 1024
FF_TILE = 256
FFN_ROW_TILE = 1024
IN_ROW_TILE = 1024
IN_COLS = RW_COLS + DA_COLS + SSM_W
IN_TILE = IN_COLS // 3
SSM_COL0 = RW_COLS + DA_COLS
ROW_TILE = 512
RW_CHUNK = 64
RW_TSTEP = 512
RW_STATE_HEADS = 4
RW_MAP_HEADS = 4
RW_MAP_TSTEP = 512
ATT_TILE = 512
SSM_TSTEP = 512
NEG_BIG = -1e30


def _dot(a, b):
    return jnp.dot(a, b, preferred_element_type=F32)


def _dot_nt(a, b):
    return lax.dot_general(a, b, (((1,), (1,)), ((), ())), preferred_element_type=F32)


def _rms(x, g, eps):
    return x * lax.rsqrt(jnp.mean(x * x, axis=-1, keepdims=True) + eps) * g


def _params(sem):
    return pltpu.CompilerParams(dimension_semantics=sem, vmem_limit_bytes=VMEM_LIMIT)


def _ff_offset(j, base=0):
    return (base // LANES + jnp.minimum(j * (FF_TILE // LANES), (D_FF - FF_TILE) // LANES)) * LANES


def _ffn_kernel(*refs, emit_next):
    if emit_next:
        x_ref, pre_ref, wg_ref, wu_ref, wd_ref, post_ref, ng_ref, o_ref, nx_ref, xn_ref = refs
    else:
        x_ref, pre_ref, wg_ref, wu_ref, wd_ref, post_ref, o_ref, xn_ref = refs
    j = pl.program_id(1)

    @pl.when(j == 0)
    def _():
        xn_ref[...] = _rms(x_ref[...], pre_ref[...], NORM_EPS).astype(BF16)
        o_ref[...] = jnp.zeros_like(o_ref)

    xn = xn_ref[...]
    gate = _dot(xn, wg_ref[0].astype(BF16))
    up = _dot(xn, wu_ref[0].astype(BF16))
    h = gate * jax.nn.sigmoid(gate) * up
    unit = _ff_offset(j) + lax.broadcasted_iota(jnp.int32, h.shape, 1)
    h = jnp.where(unit >= j * FF_TILE, h, 0.0).astype(BF16)
    o_ref[...] += _dot(h, wd_ref[0].astype(BF16))

    @pl.when(j == pl.num_programs(1) - 1)
    def _():
        y = x_ref[...] + 0.5 * _rms(o_ref[...], post_ref[...], NORM_EPS)
        o_ref[...] = y
        if emit_next:
            nx_ref[...] = _rms(y, ng_ref[...], NORM_EPS).astype(BF16)


def _ffn(x, pre_g, w_gu, w_down, post_g, layer, next_g=None):
    m, d = x.shape
    tm, tf = min(FFN_ROW_TILE, m), FF_TILE
    emit_next = next_g is not None
    vec = pl.BlockSpec((1, d), lambda i, j: (0, 0))
    row_out = pl.BlockSpec((tm, d), lambda i, j: (i, 0))
    next_out = pl.BlockSpec((tm, d), lambda i, j: (i, 0), pipeline_mode=pl.Buffered(1))
    return pl.pallas_call(
        functools.partial(_ffn_kernel, emit_next=emit_next),
        out_shape=[jax.ShapeDtypeStruct((m, d), F32)] + [jax.ShapeDtypeStruct((m, d), BF16)] * emit_next,
        grid=(m // tm, pl.cdiv(D_FF, tf)),
        in_specs=[
            pl.BlockSpec((tm, d), lambda i, j: (i, 0)),
            vec,
            pl.BlockSpec((pl.Element(1), pl.Element(d), pl.Element(tf)), lambda i, j: (layer, 0, _ff_offset(j))),
            pl.BlockSpec((pl.Element(1), pl.Element(d), pl.Element(tf)), lambda i, j: (layer, 0, _ff_offset(j, D_FF))),
            pl.BlockSpec((pl.Element(1), pl.Element(tf), pl.Element(d)), lambda i, j: (layer, _ff_offset(j), 0)),
            vec,
        ] + [vec] * emit_next,
        out_specs=[row_out] + [next_out] * emit_next,
        scratch_shapes=[pltpu.VMEM((tm, d), BF16)],
        compiler_params=pltpu.CompilerParams(dimension_semantics=("parallel", "arbitrary"),
                                             vmem_limit_bytes=FFN_VMEM_LIMIT),
        name="ffn",
    )(x, pre_g, w_gu, w_gu, w_down, post_g, *([next_g] * emit_next))


def _inproj_kernel(xn_ref, w_ref, o_ref, wb_ref):
    @pl.when(pl.program_id(1) == 0)
    def _():
        wb_ref[...] = w_ref[0].astype(BF16)

    o_ref[...] = _dot(xn_ref[...], wb_ref[...])


def _inproj(xn, w_in, layer):
    m, d = xn.shape
    tm, tn = min(IN_ROW_TILE, m), IN_TILE
    return pl.pallas_call(
        _inproj_kernel,
        out_shape=jax.ShapeDtypeStruct((m, IN_COLS), F32),
        grid=(IN_COLS // tn, m // tm),
        in_specs=[
            pl.BlockSpec((tm, d), lambda j, i: (i, 0)),
            pl.BlockSpec((pl.Element(1), pl.Element(d), pl.Element(tn)), lambda j, i: (layer, 0, j * tn),
                         pipeline_mode=pl.Buffered(1)),
        ],
        out_specs=pl.BlockSpec((tm, tn), lambda j, i: (i, j)),
        scratch_shapes=[pltpu.VMEM((d, tn), BF16)],
        compiler_params=_params(("parallel", "arbitrary")),
        name="inproj",
    )(xn, w_in)


def _softplus(x):
    return jnp.maximum(x, 0.0) + jnp.log(1.0 + jnp.exp(-jnp.abs(x)))


def _head_sums(x, first):
    s_first = jnp.sum(jnp.where(first, x, 0.0), axis=-1, keepdims=True)
    s_all = jnp.sum(x, axis=-1, keepdims=True)
    return jnp.where(first, s_first, s_all - s_first)


_NN = ((1,), (0,))
_NT = ((1,), (1,))
_TN = ((0,), (0,))


def _bdot(a, b, dims=_NN):
    return lax.dot_general(a.astype(BF16), b.astype(BF16), (dims, ((), ())), preferred_element_type=F32)


def _split_dot(x, w):
    xh, wh = x.astype(BF16), w.astype(BF16)
    xl, wl = (x - xh.astype(F32)).astype(BF16), (w - wh.astype(F32)).astype(BF16)
    return _dot(jnp.concatenate([xh, xl, xh], axis=1), jnp.concatenate([wh, wh, wl], axis=0))


def _unit_lower_inverses(ns, rowc, colc, bd):
    c = ns[0].shape[0]
    eye = (rowc == colc).astype(F32)
    base = SUBLANES
    diag = (rowc // base) == (colc // base)
    a0 = [jnp.where(diag, n, 0.0) for n in ns]
    a2 = [_bdot(a, bd(a)) for a in a0]
    a4 = [_bdot(a, bd(a)) for a in a2]
    xs = [_bdot(eye + p, bd(eye + q)) for p, q in zip(a0, a2)]
    xs = [_bdot(p, bd(eye + q)) for p, q in zip(xs, a4)]
    m = base
    while m < c:
        off = ((rowc // (2 * m)) == (colc // (2 * m))) & ((rowc // m) != (colc // m))
        xe = [_bdot(p, bd(jnp.where(off, n, 0.0))) for p, n in zip(xs, ns)]
        xs = [p + _bdot(q, bd(p)) for p, q in zip(xs, xe)]
        m *= 2
    return xs


def _rwchunk_kernel(zr_ref, zk_ref, zv_ref, zl_ref, pr_ref, pk_ref, pv_ref, pl_ref,
                    mur_ref, muk_ref, muv_ref, mul_ref, w0_ref, w2_ref, a0_ref, a2_ref, g2_ref, kk_ref, ka_ref, rk_ref,
                    rp_o, y0_o, tm_o, sadd_o, bonus_o, g_o):
    c, kd = RW_CHUNK, RW_HD
    tstep, gw = zr_ref.shape[1:]
    nheads, nchunks = gw // kd, tstep // c
    first_row = lax.broadcasted_iota(jnp.int32, (tstep, gw), 0) == 0
    at_start = pl.program_id(2) == 0

    def token_shift(z_ref, prev_ref, mu_ref):
        z = z_ref[0]
        before = jnp.where(at_start, 0.0, prev_ref[0, SUBLANES - 1:SUBLANES, :])
        z_prev = jnp.where(first_row, before, pltpu.roll(z, 1, 0))
        return z + (z_prev - z) * mu_ref[...]

    r = token_shift(zr_ref, pr_ref, mur_ref)
    k = token_shift(zk_ref, pk_ref, muk_ref)
    v = token_shift(zv_ref, pv_ref, muv_ref)
    lora = token_shift(zl_ref, pl_ref, mul_ref)
    o1, o2 = RW_DECAY_LORA, RW_DECAY_LORA + RW_AAA_LORA
    w = -_softplus(-(w0_ref[...] + _split_dot(jnp.tanh(lora[:, :o1]), w2_ref[...]))) - 0.5
    lw = -jnp.exp(w)
    a = jax.nn.sigmoid(a0_ref[...] + _split_dot(lora[:, o1:o2], a2_ref[...]))
    g_o[0] = _split_dot(jax.nn.sigmoid(lora[:, o2:]), g2_ref[...])
    kk = k * kk_ref[...]
    k = k * (1.0 + (a - 1.0) * ka_ref[...])
    first = lax.broadcasted_iota(jnp.int32, (tstep, LANES), 1) < kd
    blocks = [slice(j * LANES, (j + 1) * LANES) for j in range(gw // LANES)]
    norm = jnp.concatenate([jnp.sqrt(_head_sums(kk[:, s] * kk[:, s], first)) for s in blocks], axis=1)
    kk = kk / jnp.maximum(norm, 1e-12)
    b = kk * a
    rkr = r * k * rk_ref[...]
    bonus_o[0] = jnp.concatenate([_head_sums(rkr[:, s], first) for s in blocks], axis=1) * v

    rowc = lax.broadcasted_iota(jnp.int32, (c, gw), 0)
    lane_head = lax.broadcasted_iota(jnp.int32, (c, gw), 1) // kd
    colc = lax.broadcasted_iota(jnp.int32, (c, gw), 1) - lane_head * kd
    strict = rowc > colc
    incl = rowc >= colc
    own_block = (lax.broadcasted_iota(jnp.int32, (gw, gw), 0) // kd) == (lax.broadcasted_iota(jnp.int32, (gw, gw), 1) // kd)

    def bd(x):
        return jnp.where(own_block, jnp.concatenate([x.astype(BF16)] * nheads, axis=0), 0.0)

    def own_blocks(p):
        return sum(jnp.where(lane_head == hh, p[hh * kd:(hh + 1) * kd, :], 0.0) for hh in range(nheads))

    tri = (lax.broadcasted_iota(jnp.int32, (c, c), 0) >= lax.broadcasted_iota(jnp.int32, (c, c), 1)).astype(BF16)
    chunk_rows = [slice(ci * c, (ci + 1) * c) for ci in range(nchunks)]
    lw_c = [lw[rows] for rows in chunk_rows]
    lw_hi = [x.astype(BF16) for x in lw_c]
    lc_hi = [_bdot(tri, h) for h in lw_hi]
    lc_lo = [_bdot(tri, x - h.astype(F32)) for x, h in zip(lw_c, lw_hi)]
    lc = [p + q for p, q in zip(lc_hi, lc_lo)]
    e_inc = [jnp.exp(x) for x in lc]
    e_inv = [jnp.exp(-x) for x in lc]
    e_end = [jnp.exp(x[c - 1:c, :] - x) for x in lc]
    at = [-kk[rows] * jnp.exp(x - w_) for rows, x, w_ in zip(chunk_rows, lc, lw_c)]
    rt = [r[rows] * e for rows, e in zip(chunk_rows, e_inc)]
    vc = [v[rows] for rows in chunk_rows]
    ar = [jnp.concatenate([p, q], axis=0) for p, q in zip(at, rt)]
    pb = [_bdot(p, bd(b[rows] * e), _NT) for p, rows, e in zip(ar, chunk_rows, e_inv)]
    pk = [_bdot(p, bd(k[rows] * e), _NT) for p, rows, e in zip(ar, chunk_rows, e_inv)]
    a_ab = [jnp.where(strict, p[:c], 0.0) for p in pb]
    r_b = [jnp.where(incl, p[c:], 0.0) for p in pb]
    a_ak = [jnp.where(strict, p[:c], 0.0) for p in pk]
    r_k = [jnp.where(incl, p[c:], 0.0) for p in pk]
    akv = [_bdot(jnp.concatenate([p, q], axis=0), bd(x)) for p, q, x in zip(a_ak, r_k, vc)]
    kev = [own_blocks(_bdot(k[rows] * e, x, _TN)) for rows, e, x in zip(chunk_rows, e_end, vc)]
    xs = _unit_lower_inverses(a_ab, rowc, colc, bd)
    za = [_bdot(x, bd(p)) for x, p in zip(xs, at)]
    zv = [_bdot(x, bd(p[:c])) for x, p in zip(xs, akv)]
    rwa = [_bdot(p, bd(q)) for p, q in zip(r_b, za)]
    rwv = [_bdot(p, bd(q)) for p, q in zip(r_b, zv)]
    tz = [_bdot(b[rows] * e, jnp.concatenate([p, q], axis=1), _TN)
          for rows, e, p, q in zip(chunk_rows, e_end, za, zv)]
    for ci in range(nchunks):
        rows = pl.ds(ci * c, c)
        rp_o[0, rows, :] = rt[ci] + rwa[ci]
        y0_o[0, rows, :] = rwv[ci] + akv[ci][c:]
        tm_o[0, rows, :] = (jnp.where(rowc == colc, jnp.broadcast_to(e_inc[ci][c - 1:c, :], (c, gw)), 0.0)
                            + own_blocks(tz[ci][:, :gw]))
        sadd_o[0, rows, :] = own_blocks(tz[ci][:, gw:]) + kev[ci]


def _rwchunk(z, mu, w0, w2, a0, a2, g2, k_k, k_a, r_k):
    b_, t_, _ = z.shape
    gw = RW_MAP_HEADS * RW_HD
    assert RW_CHUNK == RW_HD
    assert RW_DECAY_LORA + RW_AAA_LORA + RW_GATE_LORA == gw and RW_W % gw == 0
    tstep = min(RW_MAP_TSTEP, t_)
    nprev = tstep // SUBLANES
    col_r, col_k, col_v, col_l = 0, RW_W // gw, 2 * RW_W // gw, 3 * RW_W // gw
    seq = lambda c0, own: pl.BlockSpec((1, tstep, gw), lambda bb, h, i: (bb, i, c0 + own * h))
    prev = lambda c0, own: pl.BlockSpec(
        (1, SUBLANES, gw), lambda bb, h, i: (bb, jnp.maximum(i * nprev - 1, 0), c0 + own * h))
    vec = lambda c0, own: pl.BlockSpec((1, gw), lambda bb, h, i: (0, c0 + own * h))
    mat = lambda rows: pl.BlockSpec((rows, gw), lambda bb, h, i: (0, h))
    places = [(col_r, 1), (col_k, 1), (col_v, 1), (col_l, 0)]
    out = pl.BlockSpec((1, tstep, gw), lambda bb, h, i: (bb, i, h))
    return pl.pallas_call(
        _rwchunk_kernel,
        out_shape=[jax.ShapeDtypeStruct((b_, t_, RW_W), F32)] * 6,
        grid=(b_, RW_W // gw, t_ // tstep),
        in_specs=([seq(*p) for p in places] + [prev(*p) for p in places] + [vec(*p) for p in places]
                  + [vec(0, 1), mat(RW_DECAY_LORA), vec(0, 1), mat(RW_AAA_LORA), mat(RW_GATE_LORA),
                     vec(0, 1), vec(0, 1), vec(0, 1)]),
        out_specs=[out] * 6,
        compiler_params=_params(("parallel", "parallel", "parallel")),
        name="rwkv_chunk",
    )(z, z, z, z, z, z, z, z, mu, mu, mu, mu, w0, w2, a0, a2, g2, k_k, k_a, r_k)


def _rwstate_kernel(rp_ref, y0_ref, tm_ref, sadd_ref, bonus_ref, g_ref, gnw_ref, gnb_ref, o_ref, s_ref, y_ref):
    @pl.when(pl.program_id(1) == 0)
    def _():
        s_ref[...] = jnp.zeros_like(s_ref)

    c, kd = RW_CHUNK, RW_HD
    tstep, width = y_ref.shape
    gw = RW_STATE_HEADS * kd
    own_block = (lax.broadcasted_iota(jnp.int32, (gw, gw), 0) // kd) == (lax.broadcasted_iota(jnp.int32, (gw, gw), 1) // kd)
    for ci in range(tstep // c):
        rows = pl.ds(ci * c, c)
        for gi in range(width // gw):
            lanes = slice(gi * gw, (gi + 1) * gw)
            st = s_ref[:, lanes].astype(BF16)
            st_bd = jnp.where(own_block, jnp.concatenate([st] * RW_STATE_HEADS, axis=0), 0.0)
            lhs = jnp.concatenate([tm_ref[0, rows, lanes], rp_ref[0, rows, lanes]], axis=0).astype(BF16)
            res = _dot(lhs, st_bd)
            s_ref[:, lanes] = res[:c] + sadd_ref[0, rows, lanes]
            y_ref[rows, lanes] = res[c:] + y0_ref[0, rows, lanes]

    first = lax.broadcasted_iota(jnp.int32, (tstep, LANES), 1) < kd
    head_mean = lambda x: _head_sums(x, first) * (1.0 / kd)
    for j in range(width // LANES):
        lanes = slice(j * LANES, (j + 1) * LANES)
        y = y_ref[:, lanes]
        d = y - head_mean(y)
        yn = d * lax.rsqrt(head_mean(d * d) + RW_GN_EPS) * gnw_ref[:, lanes] + gnb_ref[:, lanes]
        o_ref[0, :, lanes] = (yn + bonus_ref[0, :, lanes]) * g_ref[0, :, lanes]


def _rwstate(rp, y0, tm, sadd, bonus, g, gn_w, gn_b):
    b_, t_, w_ = rp.shape
    tstep = min(RW_TSTEP, t_)
    seq = pl.BlockSpec((1, tstep, w_), lambda bb, i: (bb, i, 0))
    par = pl.BlockSpec((1, w_), lambda bb, i: (0, 0))
    return pl.pallas_call(
        _rwstate_kernel,
        out_shape=jax.ShapeDtypeStruct((b_, t_, w_), F32),
        grid=(b_, t_ // tstep),
        in_specs=[seq] * 6 + [par] * 2,
        out_specs=seq,
        scratch_shapes=[pltpu.VMEM((RW_HD, w_), F32), pltpu.VMEM((tstep, w_), F32)],
        compiler_params=_params(("parallel", "arbitrary")),
        name="rwkv_state",
    )(rp, y0, tm, sadd, bonus, g, gn_w, gn_b)


def _attn_kernel(q_ref, k_ref, v_ref, lq1_ref, lk1_ref, lq2_ref, lk2_ref, sw_ref, o_ref,
                 qs_ref, m_ref, acc_ref, *, lam_init):
    qi = pl.program_id(2)
    t = q_ref.shape[1]
    q = q_ref[0] * (DA_HD ** -0.5)
    lane = lax.broadcasted_iota(jnp.int32, q.shape, 1)
    qs_ref[0] = jnp.where(lane < DA_HD, q, 0.0).astype(BF16)
    qs_ref[1] = jnp.where(lane >= DA_HD, q, 0.0).astype(BF16)
    m_ref[...] = jnp.full_like(m_ref, NEG_BIG)
    acc_ref[...] = jnp.zeros_like(acc_ref)
    hw = q.shape[1]
    ones = jnp.ones((t, hw), BF16)
    subheads = range(2)

    def key_tiles(js, diagonal):
        kbs, vbs = [], []
        for j in js:
            rows = pl.ds(pl.multiple_of(j * t, t), t)
            kbs.append(k_ref[0, rows, :].astype(BF16))
            vbs.append(jnp.concatenate([v_ref[0, rows, :].astype(BF16), ones], axis=1))
        s = [[_dot_nt(qs_ref[c], kb) for kb in kbs] for c in subheads]
        if diagonal:
            causal = (lax.broadcasted_iota(jnp.int32, (t, t), 1) <= lax.broadcasted_iota(jnp.int32, (t, t), 0))
            for c in subheads:
                s[c][-1] = jnp.where(causal, s[c][-1], NEG_BIG)
        m_old = [m_ref[c] for c in subheads]
        m_new = []
        for c in subheads:
            m = m_old[c]
            for x in s[c]:
                m = jnp.maximum(m, jnp.max(x, axis=-1, keepdims=True))
            m_new.append(m)
        m_wide = [jnp.concatenate([m] * (t // hw), axis=1) for m in m_new]
        p = [[jnp.exp(x - m_wide[c]).astype(BF16) for x in s[c]] for c in subheads]
        pv = [sum(_dot(x, vb) for x, vb in zip(p[c], vbs)) for c in subheads]
        for c in subheads:
            alpha = jnp.exp(m_old[c] - m_new[c])
            acc_ref[c] = jnp.concatenate([alpha, alpha], axis=1) * acc_ref[c] + pv[c]
            m_ref[c] = m_new[c]

    def pair_below_diagonal(jj, carry):
        key_tiles([2 * jj, 2 * jj + 1], False)
        return carry

    lax.fori_loop(0, qi // 2, pair_below_diagonal, 0)

    @pl.when(qi % 2 == 1)
    def _():
        key_tiles([qi - 1, qi], True)

    @pl.when(qi % 2 == 0)
    def _():
        key_tiles([qi], True)

    lam = (jnp.exp(jnp.sum(lq1_ref[...] * lk1_ref[...], axis=-1, keepdims=True))
           - jnp.exp(jnp.sum(lq2_ref[...] * lk2_ref[...], axis=-1, keepdims=True)) + lam_init)
    o = acc_ref[0, :, :hw] / acc_ref[0, :, hw:] - lam * (acc_ref[1, :, :hw] / acc_ref[1, :, hw:])
    o = o * lax.rsqrt(jnp.mean(o * o, axis=-1, keepdims=True) + DA_SUBLN_EPS) * sw_ref[...]
    o_ref[0] = o * (1.0 - lam_init)


def _attn(z, lq1, lk1, lq2, lk2, subln_w, lam_init):
    b_, t_, _ = z.shape
    tq = min(ATT_TILE, t_)
    hw = 2 * DA_HD
    q0, k0, v0 = RW_COLS // hw, (RW_COLS + DA_W) // hw, (RW_COLS + 2 * DA_W) // hw
    small = lambda n: pl.BlockSpec((1, n), lambda b, h, qi: (0, 0))
    return pl.pallas_call(
        functools.partial(_attn_kernel, lam_init=lam_init),
        out_shape=jax.ShapeDtypeStruct((b_, t_, DA_W), F32),
        grid=(b_, DA_HEADS, t_ // tq),
        in_specs=[
            pl.BlockSpec((1, tq, hw), lambda b, h, qi: (b, qi, q0 + h)),
            pl.BlockSpec((1, t_, hw), lambda b, h, qi: (b, 0, k0 + h)),
            pl.BlockSpec((1, t_, hw), lambda b, h, qi: (b, 0, v0 + h)),
            small(DA_HD), small(DA_HD), small(DA_HD), small(DA_HD), small(hw),
        ],
        out_specs=pl.BlockSpec((1, tq, hw), lambda b, h, qi: (b, qi, h)),
        scratch_shapes=[pltpu.VMEM((2, tq, hw), BF16), pltpu.VMEM((2, tq, hw), F32),
                        pltpu.VMEM((2, tq, 2 * hw), F32)],
        compiler_params=_params(("parallel", "parallel", "arbitrary")),
        name="diff_attn",
    )(z, z, z, lq1, lk1, lq2, lk2, subln_w)


def _gelu_tanh(x):
    return 0.5 * x * (1.0 + jnp.tanh(math.sqrt(2.0 / math.pi) * (x + 0.044715 * (x * x * x))))


def _s5_kernel(u_ref, bmat_ref, pw_ref, step_ref, cmat_ref, d_ref, wglu_ref, bglu_ref, o_ref,
               xs_ref, carry_ref):
    @pl.when(pl.program_id(1) == 0)
    def _():
        carry_ref[...] = jnp.zeros_like(carry_ref)

    u = u_ref[0]
    tt = u.shape[0]
    n = SSM_N
    hw, hn = SSM_W // 2, n // 2
    ub = u.astype(BF16)
    for h in range(2):
        ch = slice(h * hw, (h + 1) * hw)
        for part in (0, n):
            st = slice(part + h * hn, part + (h + 1) * hn)
            xs_ref[:, st] = _dot(ub[:, ch], bmat_ref[ch, st])
    pr, pi = pw_ref[0], pw_ref[1]

    def block(i, carry):
        cr, ci = carry
        rows = pl.ds(pl.multiple_of(i * SUBLANES, SUBLANES), SUBLANES)
        xr = xs_ref[rows, :n]
        xi = xs_ref[rows, n:]
        for lvl, d in enumerate((1, 2, 4)):
            ar, ai = step_ref[lvl, :, :n], step_ref[lvl, :, n:]
            sr, si = pltpu.roll(xr, d, 0), pltpu.roll(xi, d, 0)
            xr, xi = xr + ar * sr - ai * si, xi + ar * si + ai * sr
        xr, xi = xr + pr * cr - pi * ci, xi + pr * ci + pi * cr
        xs_ref[rows, :n] = xr
        xs_ref[rows, n:] = xi
        return xr[SUBLANES - 1:, :], xi[SUBLANES - 1:, :]

    cr, ci = lax.fori_loop(0, tt // SUBLANES, block, (carry_ref[0:1, :], carry_ref[1:2, :]))
    carry_ref[0:1, :] = cr
    carry_ref[1:2, :] = ci
    ys = []
    for h in range(2):
        ch = slice(h * hw, (h + 1) * hw)
        re, im = slice(h * hn, (h + 1) * hn), slice(n + h * hn, n + (h + 1) * hn)
        ys.append(_dot(xs_ref[:, re].astype(BF16), cmat_ref[re, ch]) + _dot(xs_ref[:, im].astype(BF16), cmat_ref[im, ch]))
    y = jnp.concatenate(ys, axis=1) + d_ref[...] * u
    y = _gelu_tanh(y)
    o_ref[0] = y * jax.nn.sigmoid(_dot(y.astype(BF16), wglu_ref[...]) + bglu_ref[...])


def _s5(z, bmat, pw, step, cmat, d_skip, w_glu, b_glu):
    b_, t_, _ = z.shape
    tt = min(SSM_TSTEP, t_)
    n = SSM_N
    full = lambda shape: pl.BlockSpec(shape, lambda b, i: (0,) * len(shape))
    return pl.pallas_call(
        _s5_kernel,
        out_shape=jax.ShapeDtypeStruct((b_, t_, SSM_W), F32),
        grid=(b_, t_ // tt),
        in_specs=[
            pl.BlockSpec((pl.Element(1), pl.Element(tt), pl.Element(SSM_W)), lambda b, i: (b, i * tt, SSM_COL0)),
            full((SSM_W, 2 * n)), full((2, SUBLANES, n)), full((3, SUBLANES, 2 * n)), full((2 * n, SSM_W)),
            full((1, SSM_W)), full((SSM_W, SSM_W)), full((1, SSM_W)),
        ],
        out_specs=pl.BlockSpec((1, tt, SSM_W), lambda b, i: (b, i, 0)),
        scratch_shapes=[pltpu.VMEM((tt, 2 * n), F32), pltpu.VMEM((2, n), F32)],
        compiler_params=_params(("parallel", "arbitrary")),
        name="s5",
    )(z, bmat, pw, step, cmat, d_skip, w_glu, b_glu)


def _s5_discretise(a_re, a_im, log_dt, b_re, b_im, c_re, c_im):
    g_, n_, c_ = SSM_GROUPS, SSM_STATE, SSM_GROUP
    dt = jnp.exp(log_dt)[:, None]
    mag = jnp.exp(dt * a_re)
    abar_r, abar_i = mag * jnp.cos(dt * a_im), mag * jnp.sin(dt * a_im)
    den = a_re * a_re + a_im * a_im
    nr, ni = abar_r - 1.0, abar_i
    coef_r, coef_i = (nr * a_re + ni * a_im) / den, (ni * a_re - nr * a_im) / den
    bbar_r = coef_r[..., None] * b_re - coef_i[..., None] * b_im
    bbar_i = coef_r[..., None] * b_im + coef_i[..., None] * b_re
    eye = jnp.eye(g_, dtype=F32)
    bd_in = lambda m: jnp.einsum('gnc,gh->gchn', m, eye).reshape(g_ * c_, g_ * n_)
    bmat = jnp.concatenate([bd_in(bbar_r), bd_in(bbar_i)], axis=1).astype(BF16)
    bd_out = lambda m: jnp.einsum('gcn,gh->gnhc', m, eye).reshape(g_ * n_, g_ * c_)
    cmat = jnp.concatenate([bd_out(c_re), -bd_out(c_im)], axis=0).astype(BF16)
    ar, ai = abar_r.reshape(1, -1), abar_i.reshape(1, -1)
    pows_r, pows_i = [ar], [ai]
    for _ in range(SUBLANES - 1):
        pr, pi = pows_r[-1], pows_i[-1]
        pows_r.append(pr * ar - pi * ai)
        pows_i.append(pr * ai + pi * ar)
    pw = jnp.stack([jnp.concatenate(pows_r, axis=0), jnp.concatenate(pows_i, axis=0)])
    t_idx = jnp.arange(SUBLANES)[:, None]
    step = jnp.stack([jnp.where(t_idx >= d, jnp.concatenate([pows_r[d - 1], pows_i[d - 1]], axis=1), 0.0)
                      for d in (1, 2, 4)])
    return bmat, pw, step, cmat


def _outproj_kernel(x_ref, yrw_ref, yda_ref, yss_ref, w1_ref, w2_ref, w3_ref, g_ref, o_ref):
    y = (_dot(yrw_ref[...].astype(BF16), w1_ref[...]) + _dot(yda_ref[...].astype(BF16), w2_ref[...])
         + _dot(yss_ref[...].astype(BF16), w3_ref[...]))
    o_ref[...] = x_ref[...] + _rms(y, g_ref[...], NORM_EPS)


def _outproj(x, y_rw, y_da, y_ss, w, g):
    m, d = x.shape
    tm = min(ROW_TILE, m)
    rowblk = lambda n: pl.BlockSpec((tm, n), lambda i: (i, 0))
    assert RW_W == DA_W and (RW_W + DA_W) % SSM_W == 0
    band = lambda rows, idx: pl.BlockSpec((rows, d), lambda i: (idx, 0))
    return pl.pallas_call(
        _outproj_kernel,
        out_shape=jax.ShapeDtypeStruct((m, d), F32),
        grid=(m // tm,),
        in_specs=[rowblk(d), rowblk(RW_W), rowblk(DA_W), rowblk(SSM_W),
                  band(RW_W, 0), band(DA_W, 1), band(SSM_W, (RW_W + DA_W) // SSM_W),
                  pl.BlockSpec((1, d), lambda i: (0, 0))],
        out_specs=rowblk(d),
        compiler_params=_params(("parallel",)),
        name="outproj",
    )(x, y_rw, y_da, y_ss, w, w, w, g)


def kernel(x, ffn1_pre_g, ffn1_w_gu, ffn1_w_down, ffn1_post_g, mix_pre_g, w_in, rw_mu, rw_w0, rw_w2, rw_a0, rw_a2, rw_g2, rw_k_k, rw_k_a, rw_r_k, rw_gn_w, rw_gn_b, da_lq1, da_lk1, da_lq2, da_lk2, da_subln_w, ssm_a_re, ssm_a_im, ssm_log_dt, ssm_b_re, ssm_b_im, ssm_c_re, ssm_c_im, ssm_d, ssm_w_glu, ssm_b_glu, w_out, mix_post_g, ffn2_pre_g, ffn2_w_gu, ffn2_w_down, ffn2_post_g):
    b_, t_, d_ = x.shape
    m = b_ * t_
    row = lambda a: a.reshape(1, -1)
    xf = x.reshape(m, d_)
    for l in range(DEPTH):
        xf, xn = _ffn(xf, row(ffn1_pre_g[l]), ffn1_w_gu, ffn1_w_down, row(ffn1_post_g[l]), l, row(mix_pre_g[l]))

        z = _inproj(xn, w_in, l).reshape(b_, t_, IN_COLS)

        maps = _rwchunk(z, row(rw_mu[l]), row(rw_w0[l]), rw_w2[l], row(rw_a0[l]), rw_a2[l], rw_g2[l],
                        row(rw_k_k[l]), row(rw_k_a[l]), row(rw_r_k[l]))
        y_rw = _rwstate(*maps, row(rw_gn_w[l]), row(rw_gn_b[l]))

        lam_init = 0.8 - 0.6 * math.exp(-0.3 * l)
        y_da = _attn(z, row(da_lq1[l]), row(da_lk1[l]), row(da_lq2[l]), row(da_lk2[l]),
                     row(da_subln_w[l]), lam_init)

        bmat, pw, step, cmat = _s5_discretise(ssm_a_re[l], ssm_a_im[l], ssm_log_dt[l], ssm_b_re[l],
                                              ssm_b_im[l], ssm_c_re[l], ssm_c_im[l])
        y_ss = _s5(z, bmat, pw, step, cmat, row(ssm_d[l]), ssm_w_glu[l].astype(BF16), row(ssm_b_glu[l]))

        xf = _outproj(xf, y_rw.reshape(m, RW_W), y_da.reshape(m, DA_W), y_ss.reshape(m, SSM_W),
                      w_out[l].astype(BF16), row(mix_post_g[l]))

        xf, = _ffn(xf, row(ffn2_pre_g[l]), ffn2_w_gu, ffn2_w_down, row(ffn2_post_g[l]), l)
    return xf.reshape(b_, t_, d_)
```

```python
import functools
import math

import jax
import jax.numpy as jnp
from jax import lax
from jax.experimental import pallas as pl
from jax.experimental.pallas import tpu as pltpu

F32 = jnp.float32
BF16 = jnp.bfloat16

D_MODEL = 2048
DEPTH = 2
RW_HEADS = 12
RW_HD = 64
RW_W = RW_HEADS * RW_HD
RW_DECAY_LORA = 64
RW_AAA_LORA = 64
RW_GATE_LORA = 128
RW_GN_EPS = 64e-5
DA_HEADS = 6
DA_HD = 64
DA_W = DA_HEADS * 2 * DA_HD
DA_SUBLN_EPS = 1e-5
SSM_W = D_MODEL - RW_W - DA_W
SSM_GROUP = 16
SSM_GROUPS = SSM_W // SSM_GROUP
SSM_STATE = 64
SSM_N = SSM_GROUPS * SSM_STATE
RW_COLS = 3 * RW_W + RW_DECAY_LORA + RW_AAA_LORA + RW_GATE_LORA
DA_COLS = 3 * DA_W
D_FF = 5504
NORM_EPS = 1e-6

LANES = 128
SUBLANES = 8
VMEM_LIMIT = 56 * 1024 * 1024
FFN_VMEM_LIMIT = 60 * 1024 * 1024
FF_TILE = 256
FFN_ROW_TILE = 1024
IN_ROW_TILE = 1024
IN_COLS = RW_COLS + DA_COLS + SSM_W
IN_TILE = IN_COLS // 3
SSM_COL0 = RW_COLS + DA_COLS
ROW_TILE = 512
RW_CHUNK = 64
RW_TSTEP = 512
RW_STATE_HEADS = 4
RW_MAP_HEADS = 4
RW_MAP_TSTEP = 512
ATT_TILE = 512
SSM_TSTEP = 512
NEG_BIG = -1e30


def _dot(a, b):
    return jnp.dot(a, b, preferred_element_type=F32)


def _dot_nt(a, b):
    return lax.dot_general(a, b, (((1,), (1,)), ((), ())), preferred_element_type=F32)


def _rms(x, g, eps):
    return x * lax.rsqrt(jnp.mean(x * x, axis=-1, keepdims=True) + eps) * g


def _params(sem):
    return pltpu.CompilerParams(dimension_semantics=sem, vmem_limit_bytes=VMEM_LIMIT)


def _ff_offset(j, base=0):
    return (base // LANES + jnp.minimum(j * (FF_TILE // LANES), (D_FF - FF_TILE) // LANES)) * LANES


def _ffn_kernel(*refs, emit_next):
    if emit_next:
        x_ref, pre_ref, wg_ref, wu_ref, wd_ref, post_ref, ng_ref, o_ref, nx_ref, xn_ref = refs
    else:
        x_ref, pre_ref, wg_ref, wu_ref, wd_ref, post_ref, o_ref, xn_ref = refs
    j = pl.program_id(1)

    @pl.when(j == 0)
    def _():
        xn_ref[...] = _rms(x_ref[...], pre_ref[...], NORM_EPS).astype(BF16)
        o_ref[...] = jnp.zeros_like(o_ref)

    xn = xn_ref[...]
    gate = _dot(xn, wg_ref[0].astype(BF16))
    up = _dot(xn, wu_ref[0].astype(BF16))
    h = gate * jax.nn.sigmoid(gate) * up
    unit = _ff_offset(j) + lax.broadcasted_iota(jnp.int32, h.shape, 1)
    h = jnp.where(unit >= j * FF_TILE, h, 0.0).astype(BF16)
    o_ref[...] += _dot(h, wd_ref[0].astype(BF16))

    @pl.when(j == pl.num_programs(1) - 1)
    def _():
        y = x_ref[...] + 0.5 * _rms(o_ref[...], post_ref[...], NORM_EPS)
        o_ref[...] = y
        if emit_next:
            nx_ref[...] = _rms(y, ng_ref[...], NORM_EPS).astype(BF16)


def _ffn(x, pre_g, w_gu, w_down, post_g, layer, next_g=None):
    m, d = x.shape
    tm, tf = min(FFN_ROW_TILE, m), FF_TILE
    emit_next = next_g is not None
    vec = pl.BlockSpec((1, d), lambda i, j: (0, 0))
    row_out = pl.BlockSpec((tm, d), lambda i, j: (i, 0))
    return pl.pallas_call(
        functools.partial(_ffn_kernel, emit_next=emit_next),
        out_shape=[jax.ShapeDtypeStruct((m, d), F32)] + [jax.ShapeDtypeStruct((m, d), BF16)] * emit_next,
        grid=(m // tm, pl.cdiv(D_FF, tf)),
        in_specs=[
            pl.BlockSpec((tm, d), lambda i, j: (i, 0)),
            vec,
            pl.BlockSpec((pl.Element(1), pl.Element(d), pl.Element(tf)), lambda i, j: (layer, 0, _ff_offset(j))),
            pl.BlockSpec((pl.Element(1), pl.Element(d), pl.Element(tf)), lambda i, j: (layer, 0, _ff_offset(j, D_FF))),
            pl.BlockSpec((pl.Element(1), pl.Element(tf), pl.Element(d)), lambda i, j: (layer, _ff_offset(j), 0)),
            vec,
        ] + [vec] * emit_next,
        out_specs=[row_out] * (1 + emit_next),
        scratch_shapes=[pltpu.VMEM((tm, d), BF16)],
        compiler_params=pltpu.CompilerParams(dimension_semantics=("parallel", "arbitrary"),
                                             vmem_limit_bytes=FFN_VMEM_LIMIT),
        name="ffn",
    )(x, pre_g, w_gu, w_gu, w_down, post_g, *([next_g] * emit_next))


def _inproj_kernel(xn_ref, w_ref, o_ref, wb_ref):
    @pl.when(pl.program_id(1) == 0)
    def _():
        wb_ref[...] = w_ref[0].astype(BF16)

    o_ref[...] = _dot(xn_ref[...], wb_ref[...])


def _inproj(xn, w_in, layer):
    m, d = xn.shape
    tm, tn = min(IN_ROW_TILE, m), IN_TILE
    return pl.pallas_call(
        _inproj_kernel,
        out_shape=jax.ShapeDtypeStruct((m, IN_COLS), F32),
        grid=(IN_COLS // tn, m // tm),
        in_specs=[
            pl.BlockSpec((tm, d), lambda j, i: (i, 0)),
            pl.BlockSpec((pl.Element(1), pl.Element(d), pl.Element(tn)), lambda j, i: (layer, 0, j * tn),
                         pipeline_mode=pl.Buffered(1)),
        ],
        out_specs=pl.BlockSpec((tm, tn), lambda j, i: (i, j)),
        scratch_shapes=[pltpu.VMEM((d, tn), BF16)],
        compiler_params=_params(("parallel", "arbitrary")),
        name="inproj",
    )(xn, w_in)


def _softplus(x):
    return jnp.maximum(x, 0.0) + jnp.log(1.0 + jnp.exp(-jnp.abs(x)))


def _head_sums(x, first):
    s_first = jnp.sum(jnp.where(first, x, 0.0), axis=-1, keepdims=True)
    s_all = jnp.sum(x, axis=-1, keepdims=True)
    return jnp.where(first, s_first, s_all - s_first)


_NN = ((1,), (0,))
_NT = ((1,), (1,))
_TN = ((0,), (0,))


def _bdot(a, b, dims=_NN):
    return lax.dot_general(a.astype(BF16), b.astype(BF16), (dims, ((), ())), preferred_element_type=F32)


def _split_dot(x, w):
    xh, wh = x.astype(BF16), w.astype(BF16)
    xl, wl = (x - xh.astype(F32)).astype(BF16), (w - wh.astype(F32)).astype(BF16)
    return _dot(jnp.concatenate([xh, xl, xh], axis=1), jnp.concatenate([wh, wh, wl], axis=0))


def _unit_lower_inverses(ns, rowc, colc, bd):
    c = ns[0].shape[0]
    eye = (rowc == colc).astype(F32)
    base = SUBLANES
    diag = (rowc // base) == (colc // base)
    a0 = [jnp.where(diag, n, 0.0) for n in ns]
    a2 = [_bdot(a, bd(a)) for a in a0]
    a4 = [_bdot(a, bd(a)) for a in a2]
    xs = [_bdot(eye + p, bd(eye + q)) for p, q in zip(a0, a2)]
    xs = [_bdot(p, bd(eye + q)) for p, q in zip(xs, a4)]
    m = base
    while m < c:
        off = ((rowc // (2 * m)) == (colc // (2 * m))) & ((rowc // m) != (colc // m))
        xe = [_bdot(p, bd(jnp.where(off, n, 0.0))) for p, n in zip(xs, ns)]
        xs = [p + _bdot(q, bd(p)) for p, q in zip(xs, xe)]
        m *= 2
    return xs


def _rwchunk_kernel(zr_ref, zk_ref, zv_ref, zl_ref, pr_ref, pk_ref, pv_ref, pl_ref,
                    mur_ref, muk_ref, muv_ref, mul_ref, w0_ref, w2_ref, a0_ref, a2_ref, g2_ref, kk_ref, ka_ref, rk_ref,
                    rp_o, y0_o, tm_o, sadd_o, bonus_o, g_o):
    c, kd = RW_CHUNK, RW_HD
    tstep, gw = zr_ref.shape[1:]
    nheads, nchunks = gw // kd, tstep // c
    first_row = lax.broadcasted_iota(jnp.int32, (tstep, gw), 0) == 0
    at_start = pl.program_id(2) == 0

    def token_shift(z_ref, prev_ref, mu_ref):
        z = z_ref[0]
        before = jnp.where(at_start, 0.0, prev_ref[0, SUBLANES - 1:SUBLANES, :])
        z_prev = jnp.where(first_row, before, pltpu.roll(z, 1, 0))
        return z + (z_prev - z) * mu_ref[...]

    r = token_shift(zr_ref, pr_ref, mur_ref)
    k = token_shift(zk_ref, pk_ref, muk_ref)
    v = token_shift(zv_ref, pv_ref, muv_ref)
    lora = token_shift(zl_ref, pl_ref, mul_ref)
    o1, o2 = RW_DECAY_LORA, RW_DECAY_LORA + RW_AAA_LORA
    w = -_softplus(-(w0_ref[...] + _split_dot(jnp.tanh(lora[:, :o1]), w2_ref[...]))) - 0.5
    lw = -jnp.exp(w)
    a = jax.nn.sigmoid(a0_ref[...] + _split_dot(lora[:, o1:o2], a2_ref[...]))
    g_o[0] = _split_dot(jax.nn.sigmoid(lora[:, o2:]), g2_ref[...])
    kk = k * kk_ref[...]
    k = k * (1.0 + (a - 1.0) * ka_ref[...])
    first = lax.broadcasted_iota(jnp.int32, (tstep, LANES), 1) < kd
    blocks = [slice(j * LANES, (j + 1) * LANES) for j in range(gw // LANES)]
    norm = jnp.concatenate([jnp.sqrt(_head_sums(kk[:, s] * kk[:, s], first)) for s in blocks], axis=1)
    kk = kk / jnp.maximum(norm, 1e-12)
    b = kk * a
    rkr = r * k * rk_ref[...]
    bonus_o[0] = jnp.concatenate([_head_sums(rkr[:, s], first) for s in blocks], axis=1) * v

    rowc = lax.broadcasted_iota(jnp.int32, (c, gw), 0)
    lane_head = lax.broadcasted_iota(jnp.int32, (c, gw), 1) // kd
    colc = lax.broadcasted_iota(jnp.int32, (c, gw), 1) - lane_head * kd
    strict = rowc > colc
    incl = rowc >= colc
    own_block = (lax.broadcasted_iota(jnp.int32, (gw, gw), 0) // kd) == (lax.broadcasted_iota(jnp.int32, (gw, gw), 1) // kd)

    def bd(x):
        return jnp.where(own_block, jnp.concatenate([x.astype(BF16)] * nheads, axis=0), 0.0)

    def own_blocks(p):
        return sum(jnp.where(lane_head == hh, p[hh * kd:(hh + 1) * kd, :], 0.0) for hh in range(nheads))

    tri = (lax.broadcasted_iota(jnp.int32, (c, c), 0) >= lax.broadcasted_iota(jnp.int32, (c, c), 1)).astype(BF16)
    chunk_rows = [slice(ci * c, (ci + 1) * c) for ci in range(nchunks)]
    lw_c = [lw[rows] for rows in chunk_rows]
    lw_hi = [x.astype(BF16) for x in lw_c]
    lc_hi = [_bdot(tri, h) for h in lw_hi]
    lc_lo = [_bdot(tri, x - h.astype(F32)) for x, h in zip(lw_c, lw_hi)]
    lc = [p + q for p, q in zip(lc_hi, lc_lo)]
    e_inc = [jnp.exp(x) for x in lc]
    e_inv = [jnp.exp(-x) for x in lc]
    e_end = [jnp.exp(x[c - 1:c, :] - x) for x in lc]
    at = [-kk[rows] * jnp.exp(x - w_) for rows, x, w_ in zip(chunk_rows, lc, lw_c)]
    rt = [r[rows] * e for rows, e in zip(chunk_rows, e_inc)]
    vc = [v[rows] for rows in chunk_rows]
    ar = [jnp.concatenate([p, q], axis=0) for p, q in zip(at, rt)]
    pb = [_bdot(p, bd(b[rows] * e), _NT) for p, rows, e in zip(ar, chunk_rows, e_inv)]
    pk = [_bdot(p, bd(k[rows] * e), _NT) for p, rows, e in zip(ar, chunk_rows, e_inv)]
    a_ab = [jnp.where(strict, p[:c], 0.0) for p in pb]
    r_b = [jnp.where(incl, p[c:], 0.0) for p in pb]
    a_ak = [jnp.where(strict, p[:c], 0.0) for p in pk]
    r_k = [jnp.where(incl, p[c:], 0.0) for p in pk]
    akv = [_bdot(jnp.concatenate([p, q], axis=0), bd(x)) for p, q, x in zip(a_ak, r_k, vc)]
    kev = [own_blocks(_bdot(k[rows] * e, x, _TN)) for rows, e, x in zip(chunk_rows, e_end, vc)]
    xs = _unit_lower_inverses(a_ab, rowc, colc, bd)
    za = [_bdot(x, bd(p)) for x, p in zip(xs, at)]
    zv = [_bdot(x, bd(p[:c])) for x, p in zip(xs, akv)]
    rwa = [_bdot(p, bd(q)) for p, q in zip(r_b, za)]
    rwv = [_bdot(p, bd(q)) for p, q in zip(r_b, zv)]
    tz = [_bdot(b[rows] * e, jnp.concatenate([p, q], axis=1), _TN)
          for rows, e, p, q in zip(chunk_rows, e_end, za, zv)]
    for ci in range(nchunks):
        rows = pl.ds(ci * c, c)
        rp_o[0, rows, :] = rt[ci] + rwa[ci]
        y0_o[0, rows, :] = rwv[ci] + akv[ci][c:]
        tm_o[0, rows, :] = (jnp.where(rowc == colc, jnp.broadcast_to(e_inc[ci][c - 1:c, :], (c, gw)), 0.0)
                            + own_blocks(tz[ci][:, :gw]))
        sadd_o[0, rows, :] = own_blocks(tz[ci][:, gw:]) + kev[ci]


def _rwchunk(z, mu, w0, w2, a0, a2, g2, k_k, k_a, r_k):
    b_, t_, _ = z.shape
    gw = RW_MAP_HEADS * RW_HD
    assert RW_CHUNK == RW_HD
    assert RW_DECAY_LORA + RW_AAA_LORA + RW_GATE_LORA == gw and RW_W % gw == 0
    tstep = min(RW_MAP_TSTEP, t_)
    nprev = tstep // SUBLANES
    col_r, col_k, col_v, col_l = 0, RW_W // gw, 2 * RW_W // gw, 3 * RW_W // gw
    seq = lambda c0, own: pl.BlockSpec((1, tstep, gw), lambda bb, h, i: (bb, i, c0 + own * h))
    prev = lambda c0, own: pl.BlockSpec(
        (1, SUBLANES, gw), lambda bb, h, i: (bb, jnp.maximum(i * nprev - 1, 0), c0 + own * h))
    vec = lambda c0, own: pl.BlockSpec((1, gw), lambda bb, h, i: (0, c0 + own * h))
    mat = lambda rows: pl.BlockSpec((rows, gw), lambda bb, h, i: (0, h))
    places = [(col_r, 1), (col_k, 1), (col_v, 1), (col_l, 0)]
    out = pl.BlockSpec((1, tstep, gw), lambda bb, h, i: (bb, i, h))
    return pl.pallas_call(
        _rwchunk_kernel,
        out_shape=[jax.ShapeDtypeStruct((b_, t_, RW_W), F32)] * 6,
        grid=(b_, RW_W // gw, t_ // tstep),
        in_specs=([seq(*p) for p in places] + [prev(*p) for p in places] + [vec(*p) for p in places]
                  + [vec(0, 1), mat(RW_DECAY_LORA), vec(0, 1), mat(RW_AAA_LORA), mat(RW_GATE_LORA),
                     vec(0, 1), vec(0, 1), vec(0, 1)]),
        out_specs=[out] * 6,
        compiler_params=_params(("parallel", "parallel", "parallel")),
        name="rwkv_chunk",
    )(z, z, z, z, z, z, z, z, mu, mu, mu, mu, w0, w2, a0, a2, g2, k_k, k_a, r_k)


def _rwstate_kernel(rp_ref, y0_ref, tm_ref, sadd_ref, bonus_ref, g_ref, gnw_ref, gnb_ref, o_ref, s_ref, y_ref):
    @pl.when(pl.program_id(1) == 0)
    def _():
        s_ref[...] = jnp.zeros_like(s_ref)

    c, kd = RW_CHUNK, RW_HD
    tstep, width = y_ref.shape
    gw = RW_STATE_HEADS * kd
    own_block = (lax.broadcasted_iota(jnp.int32, (gw, gw), 0) // kd) == (lax.broadcasted_iota(jnp.int32, (gw, gw), 1) // kd)
    for ci in range(tstep // c):
        rows = pl.ds(ci * c, c)
        for gi in range(width // gw):
            lanes = slice(gi * gw, (gi + 1) * gw)
            st = s_ref[:, lanes].astype(BF16)
            st_bd = jnp.where(own_block, jnp.concatenate([st] * RW_STATE_HEADS, axis=0), 0.0)
            lhs = jnp.concatenate([tm_ref[0, rows, lanes], rp_ref[0, rows, lanes]], axis=0).astype(BF16)
            res = _dot(lhs, st_bd)
            s_ref[:, lanes] = res[:c] + sadd_ref[0, rows, lanes]
            y_ref[rows, lanes] = res[c:] + y0_ref[0, rows, lanes]

    first = lax.broadcasted_iota(jnp.int32, (tstep, LANES), 1) < kd
    head_mean = lambda x: _head_sums(x, first) * (1.0 / kd)
    for j in range(width // LANES):
        lanes = slice(j * LANES, (j + 1) * LANES)
        y = y_ref[:, lanes]
        d = y - head_mean(y)
        yn = d * lax.rsqrt(head_mean(d * d) + RW_GN_EPS) * gnw_ref[:, lanes] + gnb_ref[:, lanes]
        o_ref[0, :, lanes] = (yn + bonus_ref[0, :, lanes]) * g_ref[0, :, lanes]


def _rwstate(rp, y0, tm, sadd, bonus, g, gn_w, gn_b):
    b_, t_, w_ = rp.shape
    tstep = min(RW_TSTEP, t_)
    seq = pl.BlockSpec((1, tstep, w_), lambda bb, i: (bb, i, 0))
    par = pl.BlockSpec((1, w_), lambda bb, i: (0, 0))
    return pl.pallas_call(
        _rwstate_kernel,
        out_shape=jax.ShapeDtypeStruct((b_, t_, w_), F32),
        grid=(b_, t_ // tstep),
        in_specs=[seq] * 6 + [par] * 2,
        out_specs=seq,
        scratch_shapes=[pltpu.VMEM((RW_HD, w_), F32), pltpu.VMEM((tstep, w_), F32)],
        compiler_params=_params(("parallel", "arbitrary")),
        name="rwkv_state",
    )(rp, y0, tm, sadd, bonus, g, gn_w, gn_b)


def _attn_kernel(q_ref, k_ref, v_ref, lq1_ref, lk1_ref, lq2_ref, lk2_ref, sw_ref, o_ref,
                 qs_ref, m_ref, acc_ref, *, lam_init):
    qi = pl.program_id(2)
    t = q_ref.shape[1]
    q = q_ref[0] * (DA_HD ** -0.5)
    lane = lax.broadcasted_iota(jnp.int32, q.shape, 1)
    qs_ref[0] = jnp.where(lane < DA_HD, q, 0.0).astype(BF16)
    qs_ref[1] = jnp.where(lane >= DA_HD, q, 0.0).astype(BF16)
    m_ref[...] = jnp.full_like(m_ref, NEG_BIG)
    acc_ref[...] = jnp.zeros_like(acc_ref)
    hw = q.shape[1]
    ones = jnp.ones((t, hw), BF16)
    subheads = range(2)

    def key_tiles(js, diagonal):
        kbs, vbs = [], []
        for j in js:
            rows = pl.ds(pl.multiple_of(j * t, t), t)
            kbs.append(k_ref[0, rows, :].astype(BF16))
            vbs.append(jnp.concatenate([v_ref[0, rows, :].astype(BF16), ones], axis=1))
        s = [[_dot_nt(qs_ref[c], kb) for kb in kbs] for c in subheads]
        if diagonal:
            causal = (lax.broadcasted_iota(jnp.int32, (t, t), 1) <= lax.broadcasted_iota(jnp.int32, (t, t), 0))
            for c in subheads:
                s[c][-1] = jnp.where(causal, s[c][-1], NEG_BIG)
        m_old = [m_ref[c] for c in subheads]
        m_new = []
        for c in subheads:
            m = m_old[c]
            for x in s[c]:
                m = jnp.maximum(m, jnp.max(x, axis=-1, keepdims=True))
            m_new.append(m)
        m_wide = [jnp.concatenate([m] * (t // hw), axis=1) for m in m_new]
        p = [[jnp.exp(x - m_wide[c]).astype(BF16) for x in s[c]] for c in subheads]
        pv = [sum(_dot(x, vb) for x, vb in zip(p[c], vbs)) for c in subheads]
        for c in subheads:
            alpha = jnp.exp(m_old[c] - m_new[c])
            acc_ref[c] = jnp.concatenate([alpha, alpha], axis=1) * acc_ref[c] + pv[c]
            m_ref[c] = m_new[c]

    def pair_below_diagonal(jj, carry):
        key_tiles([2 * jj, 2 * jj + 1], False)
        return carry

    lax.fori_loop(0, qi // 2, pair_below_diagonal, 0)

    @pl.when(qi % 2 == 1)
    def _():
        key_tiles([qi - 1, qi], True)

    @pl.when(qi % 2 == 0)
    def _():
        key_tiles([qi], True)

    lam = (jnp.exp(jnp.sum(lq1_ref[...] * lk1_ref[...], axis=-1, keepdims=True))
           - jnp.exp(jnp.sum(lq2_ref[...] * lk2_ref[...], axis=-1, keepdims=True)) + lam_init)
    o = acc_ref[0, :, :hw] / acc_ref[0, :, hw:] - lam * (acc_ref[1, :, :hw] / acc_ref[1, :, hw:])
    o = o * lax.rsqrt(jnp.mean(o * o, axis=-1, keepdims=True) + DA_SUBLN_EPS) * sw_ref[...]
    o_ref[0] = o * (1.0 - lam_init)


def _attn(z, lq1, lk1, lq2, lk2, subln_w, lam_init):
    b_, t_, _ = z.shape
    tq = min(ATT_TILE, t_)
    hw = 2 * DA_HD
    q0, k0, v0 = RW_COLS // hw, (RW_COLS + DA_W) // hw, (RW_COLS + 2 * DA_W) // hw
    small = lambda n: pl.BlockSpec((1, n), lambda b, h, qi: (0, 0))
    return pl.pallas_call(
        functools.partial(_attn_kernel, lam_init=lam_init),
        out_shape=jax.ShapeDtypeStruct((b_, t_, DA_W), F32),
        grid=(b_, DA_HEADS, t_ // tq),
        in_specs=[
            pl.BlockSpec((1, tq, hw), lambda b, h, qi: (b, qi, q0 + h)),
            pl.BlockSpec((1, t_, hw), lambda b, h, qi: (b, 0, k0 + h)),
            pl.BlockSpec((1, t_, hw), lambda b, h, qi: (b, 0, v0 + h)),
            small(DA_HD), small(DA_HD), small(DA_HD), small(DA_HD), small(hw),
        ],
        out_specs=pl.BlockSpec((1, tq, hw), lambda b, h, qi: (b, qi, h)),
        scratch_shapes=[pltpu.VMEM((2, tq, hw), BF16), pltpu.VMEM((2, tq, hw), F32),
                        pltpu.VMEM((2, tq, 2 * hw), F32)],
        compiler_params=_params(("parallel", "parallel", "arbitrary")),
        name="diff_attn",
    )(z, z, z, lq1, lk1, lq2, lk2, subln_w)


def _gelu_tanh(x):
    return 0.5 * x * (1.0 + jnp.tanh(math.sqrt(2.0 / math.pi) * (x + 0.044715 * (x * x * x))))


def _s5_kernel(u_ref, bmat_ref, pw_ref, step_ref, cmat_ref, d_ref, wglu_ref, bglu_ref, o_ref,
               xs_ref, carry_ref):
    @pl.when(pl.program_id(1) == 0)
    def _():
        carry_ref[...] = jnp.zeros_like(carry_ref)

    u = u_ref[0]
    tt = u.shape[0]
    n = SSM_N
    hw, hn = SSM_W // 2, n // 2
    ub = u.astype(BF16)
    for h in range(2):
        ch = slice(h * hw, (h + 1) * hw)
        for part in (0, n):
            st = slice(part + h * hn, part + (h + 1) * hn)
            xs_ref[:, st] = _dot(ub[:, ch], bmat_ref[ch, st])
    pr, pi = pw_ref[0], pw_ref[1]

    def block(i, carry):
        cr, ci = carry
        rows = pl.ds(pl.multiple_of(i * SUBLANES, SUBLANES), SUBLANES)
        xr = xs_ref[rows, :n]
        xi = xs_ref[rows, n:]
        for lvl, d in enumerate((1, 2, 4)):
            ar, ai = step_ref[lvl, :, :n], step_ref[lvl, :, n:]
            sr, si = pltpu.roll(xr, d, 0), pltpu.roll(xi, d, 0)
            xr, xi = xr + ar * sr - ai * si, xi + ar * si + ai * sr
        xr, xi = xr + pr * cr - pi * ci, xi + pr * ci + pi * cr
        xs_ref[rows, :n] = xr
        xs_ref[rows, n:] = xi
        return xr[SUBLANES - 1:, :], xi[SUBLANES - 1:, :]

    cr, ci = lax.fori_loop(0, tt // SUBLANES, block, (carry_ref[0:1, :], carry_ref[1:2, :]))
    carry_ref[0:1, :] = cr
    carry_ref[1:2, :] = ci
    ys = []
    for h in range(2):
        ch = slice(h * hw, (h + 1) * hw)
        re, im = slice(h * hn, (h + 1) * hn), slice(n + h * hn, n + (h + 1) * hn)
        ys.append(_dot(xs_ref[:, re].astype(BF16), cmat_ref[re, ch]) + _dot(xs_ref[:, im].astype(BF16), cmat_ref[im, ch]))
    y = jnp.concatenate(ys, axis=1) + d_ref[...] * u
    y = _gelu_tanh(y)
    o_ref[0] = y * jax.nn.sigmoid(_dot(y.astype(BF16), wglu_ref[...]) + bglu_ref[...])


def _s5(z, bmat, pw, step, cmat, d_skip, w_glu, b_glu):
    b_, t_, _ = z.shape
    tt = min(SSM_TSTEP, t_)
    n = SSM_N
    full = lambda shape: pl.BlockSpec(shape, lambda b, i: (0,) * len(shape))
    return pl.pallas_call(
        _s5_kernel,
        out_shape=jax.ShapeDtypeStruct((b_, t_, SSM_W), F32),
        grid=(b_, t_ // tt),
        in_specs=[
            pl.BlockSpec((pl.Element(1), pl.Element(tt), pl.Element(SSM_W)), lambda b, i: (b, i * tt, SSM_COL0)),
            full((SSM_W, 2 * n)), full((2, SUBLANES, n)), full((3, SUBLANES, 2 * n)), full((2 * n, SSM_W)),
            full((1, SSM_W)), full((SSM_W, SSM_W)), full((1, SSM_W)),
        ],
        out_specs=pl.BlockSpec((1, tt, SSM_W), lambda b, i: (b, i, 0)),
        scratch_shapes=[pltpu.VMEM((tt, 2 * n), F32), pltpu.VMEM((2, n), F32)],
        compiler_params=_params(("parallel", "arbitrary")),
        name="s5",
    )(z, bmat, pw, step, cmat, d_skip, w_glu, b_glu)


def _s5_discretise(a_re, a_im, log_dt, b_re, b_im, c_re, c_im):
    g_, n_, c_ = SSM_GROUPS, SSM_STATE, SSM_GROUP
    dt = jnp.exp(log_dt)[:, None]
    mag = jnp.exp(dt * a_re)
    abar_r, abar_i = mag * jnp.cos(dt * a_im), mag * jnp.sin(dt * a_im)
    den = a_re * a_re + a_im * a_im
    nr, ni = abar_r - 1.0, abar_i
    coef_r, coef_i = (nr * a_re + ni * a_im) / den, (ni * a_re - nr * a_im) / den
    bbar_r = coef_r[..., None] * b_re - coef_i[..., None] * b_im
    bbar_i = coef_r[..., None] * b_im + coef_i[..., None] * b_re
    eye = jnp.eye(g_, dtype=F32)
    bd_in = lambda m: jnp.einsum('gnc,gh->gchn', m, eye).reshape(g_ * c_, g_ * n_)
    bmat = jnp.concatenate([bd_in(bbar_r), bd_in(bbar_i)], axis=1).astype(BF16)
    bd_out = lambda m: jnp.einsum('gcn,gh->gnhc', m, eye).reshape(g_ * n_, g_ * c_)
    cmat = jnp.concatenate([bd_out(c_re), -bd_out(c_im)], axis=0).astype(BF16)
    ar, ai = abar_r.reshape(1, -1), abar_i.reshape(1, -1)
    pows_r, pows_i = [ar], [ai]
    for _ in range(SUBLANES - 1):
        pr, pi = pows_r[-1], pows_i[-1]
        pows_r.append(pr * ar - pi * ai)
        pows_i.append(pr * ai + pi * ar)
    pw = jnp.stack([jnp.concatenate(pows_r, axis=0), jnp.concatenate(pows_i, axis=0)])
    t_idx = jnp.arange(SUBLANES)[:, None]
    step = jnp.stack([jnp.where(t_idx >= d, jnp.concatenate([pows_r[d - 1], pows_i[d - 1]], axis=1), 0.0)
                      for d in (1, 2, 4)])
    return bmat, pw, step, cmat


def _outproj_kernel(x_ref, yrw_ref, yda_ref, yss_ref, w1_ref, w2_ref, w3_ref, g_ref, o_ref):
    y = (_dot(yrw_ref[...].astype(BF16), w1_ref[...]) + _dot(yda_ref[...].astype(BF16), w2_ref[...])
         + _dot(yss_ref[...].astype(BF16), w3_ref[...]))
    o_ref[...] = x_ref[...] + _rms(y, g_ref[...], NORM_EPS)


def _outproj(x, y_rw, y_da, y_ss, w, g):
    m, d = x.shape
    tm = min(ROW_TILE, m)
    rowblk = lambda n: pl.BlockSpec((tm, n), lambda i: (i, 0))
    assert RW_W == DA_W and (RW_W + DA_W) % SSM_W == 0
    band = lambda rows, idx: pl.BlockSpec((rows, d), lambda i: (idx, 0))
    return pl.pallas_call(
        _outproj_kernel,
        out_shape=jax.ShapeDtypeStruct((m, d), F32),
        grid=(m // tm,),
        in_specs=[rowblk(d), rowblk(RW_W), rowblk(DA_W), rowblk(SSM_W),
                  band(RW_W, 0), band(DA_W, 1), band(SSM_W, (RW_W + DA_W) // SSM_W),
                  pl.BlockSpec((1, d), lambda i: (0, 0))],
        out_specs=rowblk(d),
        compiler_params=_params(("parallel",)),
        name="outproj",
    )(x, y_rw, y_da, y_ss, w, w, w, g)


def kernel(x, ffn1_pre_g, ffn1_w_gu, ffn1_w_down, ffn1_post_g, mix_pre_g, w_in, rw_mu, rw_w0, rw_w2, rw_a0, rw_a2, rw_g2, rw_k_k, rw_k_a, rw_r_k, rw_gn_w, rw_gn_b, da_lq1, da_lk1, da_lq2, da_lk2, da_subln_w, ssm_a_re, ssm_a_im, ssm_log_dt, ssm_b_re, ssm_b_im, ssm_c_re, ssm_c_im, ssm_d, ssm_w_glu, ssm_b_glu, w_out, mix_post_g, ffn2_pre_g, ffn2_w_gu, ffn2_w_down, ffn2_post_g):
    b_, t_, d_ = x.shape
    m = b_ * t_
    row = lambda a: a.reshape(1, -1)
    xf = x.reshape(m, d_)
    for l in range(DEPTH):
        xf, xn = _ffn(xf, row(ffn1_pre_g[l]), ffn1_w_gu, ffn1_w_down, row(ffn1_post_g[l]), l, row(mix_pre_g[l]))

        z = _inproj(xn, w_in, l).reshape(b_, t_, IN_COLS)

        maps = _rwchunk(z, row(rw_mu[l]), row(rw_w0[l]), rw_w2[l], row(rw_a0[l]), rw_a2[l], rw_g2[l],
                        row(rw_k_k[l]), row(rw_k_a[l]), row(rw_r_k[l]))
        y_rw = _rwstate(*maps, row(rw_gn_w[l]), row(rw_gn_b[l]))

        lam_init = 0.8 - 0.6 * math.exp(-0.3 * l)
        y_da = _attn(z, row(da_lq1[l]), row(da_lk1[l]), row(da_lq2[l]), row(da_lk2[l]),
                     row(da_subln_w[l]), lam_init)

        bmat, pw, step, cmat = _s5_discretise(ssm_a_re[l], ssm_a_im[l], ssm_log_dt[l], ssm_b_re[l],
                                              ssm_b_im[l], ssm_c_re[l], ssm_c_im[l])
        y_ss = _s5(z, bmat, pw, step, cmat, row(ssm_d[l]), ssm_w_glu[l].astype(BF16), row(ssm_b_glu[l]))

        xf = _outproj(xf, y_rw.reshape(m, RW_W), y_da.reshape(m, DA_W), y_ss.reshape(m, SSM_W),
                      w_out[l].astype(BF16), row(mix_post_g[l]))

        xf, = _ffn(xf, row(ffn2_pre_g[l]), ffn2_w_gu, ffn2_w_down, row(ffn2_post_g[l]), l)
    return xf.reshape(b_, t_, d_)
```
